```python
import jax, jax.numpy as jnp
from jax import lax
import numpy as np

D_MODEL = 4096
BATCH = 2
SEQ = 8192
DEPTH = 2
DEC_BATCH = 8
DEC_SEQ = 64
PAST_LEN = 4096

CHUNK = 64
N_BRANCH = 3
BR_WIDTH = 1024
NORM_EPS = 1e-5
MASK_VALUE = -1e30
HG_HEADS = 8
HG_DK = 128
HG_DV = BR_WIDTH // HG_HEADS
HG_BLOCK = 16
HG_IN = 4 * BR_WIDTH
RW_HEADS = 16
RW_HD = 64
RW_W_RANK = 64
RW_A_RANK = 64
RW_G_RANK = 128
RW_IN = 3 * BR_WIDTH + RW_W_RANK + RW_A_RANK + RW_G_RANK
RW_LN_EPS = 64e-5
SW_HEADS = 16
SW_KV_HEADS = 2
Q_PER_KV = SW_HEADS // SW_KV_HEADS
SW_HD = 64
WINDOW = 128
WINDOW_CHUNKS = WINDOW // CHUNK
ROPE_THETA = 10000.0
SW_IN = SW_HEADS * SW_HD + 2 * SW_KV_HEADS * SW_HD
N_IN = HG_IN + RW_IN + SW_IN
D_FF = 14336
N_EXPERTS = 8
TOP_K = 2
MOE_BLOCK = 256

kernel_name = 'hybrid_streaming_encoder_step'


def _rmsnorm(x, g):
    xf = x.astype(jnp.float32)
    y = xf * lax.rsqrt(jnp.mean(xf * xf, axis=-1, keepdims=True) + NORM_EPS)
    return (y * g.astype(jnp.float32)).astype(x.dtype)


def _rope(x, pos):
    half = x.shape[-1] // 2
    inv = ROPE_THETA ** (-jnp.arange(half, dtype=jnp.float32) / half)
    ang = pos.astype(jnp.float32)[:, None] * inv[None, :]
    cos = jnp.cos(ang)[None, :, None, :]
    sin = jnp.sin(ang)[None, :, None, :]
    xf = x.astype(jnp.float32)
    x1, x2 = xf[..., :half], xf[..., half:]
    return jnp.concatenate([x1 * cos - x2 * sin, x2 * cos + x1 * sin], axis=-1).astype(x.dtype)


def _gated_recurrence(q, k, v, log_f, s0):
    B, T, H, _ = q.shape
    L = HG_BLOCK
    nb = -(-T // L)
    pad = nb * L - T

    def blocks(a):
        a = jnp.pad(a.astype(jnp.float32), ((0, 0), (0, pad), (0, 0), (0, 0)))
        return a.reshape(B, nb, L, H, a.shape[-1]).transpose(1, 0, 3, 2, 4)

    causal = jnp.tril(jnp.ones((L, L), dtype=bool))[:, :, None]
    causal_f = causal.astype(jnp.float32)

    def step(S, blk):
        qb, kb, vb, lfb = blk
        b = jnp.cumsum(lfb, axis=2)
        o_inter = jnp.einsum('bhtc,bhcv->bhtv', qb * jnp.exp(b), S)
        diff = jnp.where(causal, b[:, :, :, None, :] - b[:, :, None, :, :], 0.0)
        dec = jnp.exp(diff) * causal_f
        attn = jnp.einsum('bhtc,bhtsc,bhsc->bhts', qb, dec, kb)
        o = o_inter + jnp.einsum('bhts,bhsv->bhtv', attn, vb)
        b_last = b[:, :, -1:, :]
        S = S * jnp.exp(b_last[:, :, 0, :, None]) + jnp.einsum('bhsc,bhsv->bhcv', kb * jnp.exp(b_last - b), vb)
        return S, o

    S, o = lax.scan(step, s0.astype(jnp.float32), (blocks(q), blocks(k), blocks(v), blocks(log_f)))
    o = o.transpose(1, 0, 3, 2, 4).reshape(B, nb * L, H, -1)[:, :T]
    return o, S


def _hgrn2(p, lb, norm_g, s0):
    B, T, _ = p.shape
    q, fl, i, g = jnp.split(p.astype(jnp.float32), 4, axis=-1)
    lb = lb.astype(jnp.float32)
    f = lb + (1.0 - lb) * jax.nn.sigmoid(fl)
    log_f = jnp.log(f)
    k = 1.0 - f
    hd = lambda a: a.reshape(B, T, HG_HEADS, -1)
    o, s = _gated_recurrence(hd(jax.nn.silu(q)), hd(k), hd(i), hd(log_f), s0)
    o = _rmsnorm(o, norm_g.reshape(HG_HEADS, HG_DV)).reshape(B, T, BR_WIDTH)
    return (o * jax.nn.silu(g)).astype(p.dtype), s.astype(s0.dtype)


def _rwkv7(p, prev, s0, mu, w0, w2, a0, a2, g2, k_k, k_a, r_k, ln_w, ln_b):
    B, T, _ = p.shape
    f32 = jnp.float32
    W = BR_WIDTH
    pf = p.astype(f32)
    shifted = jnp.concatenate([prev.astype(f32)[:, None, :], pf[:, :-1]], axis=1)
    pm = pf + (shifted - pf) * mu.astype(f32)
    r, k, v, wd, ad, gd = jnp.split(pm, [W, 2 * W, 3 * W, 3 * W + RW_W_RANK, 3 * W + RW_W_RANK + RW_A_RANK], axis=-1)
    w = -jax.nn.softplus(-(w0.astype(f32) + jnp.tanh(wd) @ w2.astype(f32))) - 0.5
    decay = jnp.exp(-jnp.exp(w))
    a = jax.nn.sigmoid(a0.astype(f32) + ad @ a2.astype(f32))
    g = jax.nn.sigmoid(gd) @ g2.astype(f32)
    hd = lambda t: t.reshape(B, T, RW_HEADS, RW_HD)
    kk = hd(k * k_k.astype(f32))
    kk = kk / jnp.maximum(jnp.sqrt(jnp.sum(kk * kk, axis=-1, keepdims=True)), 1e-12)
    k = k * (1.0 + (a - 1.0) * k_a.astype(f32))
    r, k, v, decay, a = hd(r), hd(k), hd(v), hd(decay), hd(a)

    def step(S, inp):
        r_t, w_t, k_t, v_t, kk_t, a_t = inp
        sa = jnp.einsum('bhvk,bhk->bhv', S, -kk_t)
        S = S * w_t[:, :, None, :] + sa[..., None] * (kk_t * a_t)[:, :, None, :] + v_t[..., None] * k_t[:, :, None, :]
        return S, jnp.einsum('bhvk,bhk->bhv', S, r_t)

    tm = lambda t: jnp.moveaxis(t, 1, 0)
    S, y = lax.scan(step, s0.astype(f32), (tm(r), tm(decay), tm(k), tm(v), tm(kk), tm(a)))
    y = jnp.moveaxis(y, 0, 1)
    mean = jnp.mean(y, axis=-1, keepdims=True)
    var = jnp.mean(jnp.square(y - mean), axis=-1, keepdims=True)
    y = (y - mean) * lax.rsqrt(var + RW_LN_EPS) * ln_w.astype(f32).reshape(RW_HEADS, RW_HD) + ln_b.astype(f32).reshape(RW_HEADS, RW_HD)
    y = y + jnp.sum(r * k * r_k.astype(f32), axis=-1, keepdims=True) * v
    out = y.reshape(B, T, W) * g
    return out.astype(p.dtype), S.astype(s0.dtype), p[:, -1]


def _swa(p, pos, k_past, v_past, past_valid, sinks):
    B, T, _ = p.shape
    f32 = jnp.float32
    qw, kw = SW_HEADS * SW_HD, SW_KV_HEADS * SW_HD
    q = _rope(p[..., :qw].reshape(B, T, SW_HEADS, SW_HD), pos)
    k = _rope(p[..., qw:qw + kw].reshape(B, T, SW_KV_HEADS, SW_HD), pos)
    v = p[..., qw + kw:].reshape(B, T, SW_KV_HEADS, SW_HD)
    n_c = -(-T // CHUNK)
    pad = n_c * CHUNK - T
    n_past = k_past.shape[1]
    padt = lambda a: jnp.pad(a, ((0, 0), (0, pad), (0, 0), (0, 0)))

    def band(past, cur):
        full = jnp.concatenate([past.astype(cur.dtype), padt(cur)], axis=1)
        full = full.reshape(B, WINDOW_CHUNKS + n_c, CHUNK, SW_KV_HEADS, SW_HD)
        return jnp.concatenate([full[:, j:j + n_c] for j in range(WINDOW_CHUNKS + 1)], axis=2)

    kb, vb = band(k_past, k), band(v_past, v)
    qb = padt(q).reshape(B, n_c, CHUNK, SW_KV_HEADS, Q_PER_KV, SW_HD)
    idx = jnp.arange(n_c)[:, None] * CHUNK + jnp.arange((WINDOW_CHUNKS + 1) * CHUNK)[None, :]
    valid = jnp.where(idx < n_past, past_valid, idx - n_past < T)
    s = jnp.einsum('bcqkgd,bcskd->bckgqs', qb, kb).astype(f32) * (SW_HD ** -0.5)
    s = jnp.where(valid[None, :, None, None, None, :], s, MASK_VALUE)
    sink = sinks.astype(f32).reshape(SW_KV_HEADS, Q_PER_KV)[None, None, :, :, None, None]
    m = jnp.maximum(jnp.max(s, axis=-1, keepdims=True), sink)
    e = jnp.exp(s - m)
    pr = e / (jnp.sum(e, axis=-1, keepdims=True) + jnp.exp(sink - m))
    o = jnp.einsum('bckgqs,bcskd->bcqkgd', pr, vb.astype(f32))
    o = o.reshape(B, n_c * CHUNK, SW_HEADS * SW_HD)[:, :T]
    return o.astype(p.dtype), k, v


def _swiglu(h, w_g, w_u, w_d):
    return (jax.nn.silu(h @ w_g) * (h @ w_u)) @ w_d


def _moe(h, router, w_g, w_u, w_d):
    n, d = h.shape
    f32 = jnp.float32
    n_assign = n * TOP_K
    logits = (h @ router).astype(f32)
    top_val, top_idx = lax.top_k(logits, TOP_K)
    gate = jax.nn.softmax(top_val, axis=-1).reshape(-1)
    eid = top_idx.reshape(-1)
    tok = jnp.broadcast_to(jnp.arange(n)[:, None], (n, TOP_K)).reshape(-1)
    order = jnp.argsort(eid)
    eid_s, tok_s, gate_s = eid[order], tok[order], gate[order]
    counts = jnp.bincount(eid, length=N_EXPERTS)
    padded = (counts + MOE_BLOCK - 1) // MOE_BLOCK * MOE_BLOCK
    start = jnp.cumsum(counts) - counts
    pend = jnp.cumsum(padded)
    pstart = pend - padded
    dest = pstart[eid_s] + jnp.arange(n_assign) - start[eid_s]
    n_blocks = -(-n_assign // MOE_BLOCK) + N_EXPERTS
    rows = jnp.zeros((n_blocks * MOE_BLOCK, d), h.dtype).at[dest].set(h[tok_s])
    block_e = jnp.minimum(jnp.searchsorted(pend, jnp.arange(n_blocks) * MOE_BLOCK, side='right'), N_EXPERTS - 1)

    def expert_block(args):
        xb, e = args
        return (jax.nn.silu(xb @ w_g[e]) * (xb @ w_u[e])) @ w_d[e]

    out = lax.map(expert_block, (rows.reshape(n_blocks, MOE_BLOCK, d), block_e)).reshape(-1, d)
    contrib = (out[dest].astype(f32) * gate_s[:, None]).astype(h.dtype)
    return jnp.zeros((n, d), h.dtype).at[tok_s].add(contrib)


def _trunk(x, pos, hg_s0, rw_s0, rw_prev0, sw_k0, sw_v0, past_valid, prm):
    B, T, D = x.shape
    lbs = jax.nn.softmax(prm['hg_lower_bounds'].astype(jnp.float32), axis=0)
    lbs = jnp.cumsum(lbs, axis=0) - lbs[0]
    hg_out, rw_out, sh_out, k_out, v_out = [], [], [], [], []
    for l in range(DEPTH):
        h = _rmsnorm(x, prm['norm_mix'][l])
        proj = h @ prm['w_in'][l]
        p_hg = proj[..., :HG_IN]
        p_rw = proj[..., HG_IN:HG_IN + RW_IN]
        p_sw = proj[..., HG_IN + RW_IN:]
        y_hg, s_hg = _hgrn2(p_hg, lbs[l], prm['hg_norm'][l], hg_s0[l])
        y_rw, s_rw, last_rw = _rwkv7(p_rw, rw_prev0[l], rw_s0[l], prm['rw_mu'][l], prm['rw_w0'][l], prm['rw_w2'][l],
                                     prm['rw_a0'][l], prm['rw_a2'][l], prm['rw_g2'][l], prm['rw_kk'][l], prm['rw_ka'][l],
                                     prm['rw_rk'][l], prm['rw_lnw'][l], prm['rw_lnb'][l])
        y_sw, k_new, v_new = _swa(p_sw, pos, sw_k0[l], sw_v0[l], past_valid, prm['sw_sinks'][l])
        branches = jnp.stack([y_hg, y_rw, y_sw], axis=2)
        gates = jax.nn.sigmoid((h @ prm['w_gate_br'][l]).astype(jnp.float32)).reshape(B, T, N_BRANCH, D)
        merged = jnp.sum(gates * jnp.einsum('btnw,nwd->btnd', branches, prm['w_branch'][l]), axis=2)
        x = x + merged.astype(x.dtype) @ prm['w_out'][l]
        h = _rmsnorm(x, prm['norm_ffn'][l])
        j = l // 2
        if l % 2 == 0:
            f = _swiglu(h, prm['ffn_w_gate'][j], prm['ffn_w_up'][j], prm['ffn_w_down'][j])
        else:
            f = _moe(h.reshape(B * T, D), prm['moe_router'][j], prm['moe_w_gate'][j], prm['moe_w_up'][j],
                     prm['moe_w_down'][j]).reshape(B, T, D)
        x = x + f
        hg_out.append(s_hg)
        rw_out.append(s_rw)
        sh_out.append(last_rw)
        k_out.append(k_new)
        v_out.append(v_new)
    y = _rmsnorm(x, prm['norm_final'])
    return y, jnp.stack(hg_out), jnp.stack(rw_out), jnp.stack(sh_out), jnp.stack(k_out), jnp.stack(v_out)


def setup_inputs(seed: int = 0) -> dict:
    key = jax.random.key(seed)
    ks = iter(jax.random.split(key, 40))
    f32 = jnp.float32

    def nrm(shape, scale=1.0, shift=0.0):
        return jax.random.normal(next(ks), shape, f32) * scale + shift

    n_dense = (DEPTH + 1) // 2
    n_moe = DEPTH // 2
    W = BR_WIDTH
    sw_rows = min(WINDOW, PAST_LEN)
    return {
        'x_prompt': nrm((BATCH, SEQ, D_MODEL)),
        'x_sample': nrm((DEC_BATCH, DEC_SEQ, D_MODEL)),
        'state_hgrn': nrm((DEPTH, DEC_BATCH, HG_HEADS, HG_DK, HG_DV), 0.3),
        'state_rwkv': nrm((DEPTH, DEC_BATCH, RW_HEADS, RW_HD, RW_HD), 0.3),
        'state_rwkv_shift': nrm((DEPTH, DEC_BATCH, RW_IN)),
        'cache_swa_k': nrm((DEPTH, DEC_BATCH, sw_rows, SW_KV_HEADS, SW_HD)),
        'cache_swa_v': nrm((DEPTH, DEC_BATCH, sw_rows, SW_KV_HEADS, SW_HD)),
        'norm_mix': nrm((DEPTH, D_MODEL), 0.05, 1.0),
        'w_in': nrm((DEPTH, D_MODEL, N_IN), D_MODEL ** -0.5),
        'hg_lower_bounds': nrm((DEPTH, W), 0.5),
        'hg_norm': nrm((DEPTH, W), 0.05, 1.0),
        'rw_mu': jax.random.uniform(next(ks), (DEPTH, RW_IN), f32),
        'rw_w0': nrm((DEPTH, W), 0.5, -1.5),
        'rw_w2': nrm((DEPTH, RW_W_RANK, W), 0.5 * RW_W_RANK ** -0.5),
        'rw_a0': nrm((DEPTH, W), 0.5),
        'rw_a2': nrm((DEPTH, RW_A_RANK, W), 0.5 * RW_A_RANK ** -0.5),
        'rw_g2': nrm((DEPTH, RW_G_RANK, W), RW_G_RANK ** -0.5),
        'rw_kk': nrm((DEPTH, W), 0.1, 0.85),
        'rw_ka': nrm((DEPTH, W), 0.1, 1.0),
        'rw_rk': nrm((DEPTH, RW_HEADS, RW_HD), 0.1),
        'rw_lnw': nrm((DEPTH, W), 0.05, 1.0),
        'rw_lnb': nrm((DEPTH, W), 0.02),
        'sw_sinks': nrm((DEPTH, SW_HEADS), 0.5),
        'w_branch': nrm((DEPTH, N_BRANCH, W, D_MODEL), W ** -0.5),
        'w_gate_br': nrm((DEPTH, D_MODEL, N_BRANCH * D_MODEL), D_MODEL ** -0.5),
        'w_out': nrm((DEPTH, D_MODEL, D_MODEL), D_MODEL ** -0.5),
        'norm_ffn': nrm((DEPTH, D_MODEL), 0.05, 1.0),
        'ffn_w_gate': nrm((n_dense, D_MODEL, D_FF), D_MODEL ** -0.5),
        'ffn_w_up': nrm((n_dense, D_MODEL, D_FF), D_MODEL ** -0.5),
        'ffn_w_down': nrm((n_dense, D_FF, D_MODEL), D_FF ** -0.5),
        'moe_router': nrm((n_moe, D_MODEL, N_EXPERTS), D_MODEL ** -0.5),
        'moe_w_gate': nrm((n_moe, N_EXPERTS, D_MODEL, D_FF), D_MODEL ** -0.5),
        'moe_w_up': nrm((n_moe, N_EXPERTS, D_MODEL, D_FF), D_MODEL ** -0.5),
        'moe_w_down': nrm((n_moe, N_EXPERTS, D_FF, D_MODEL), D_FF ** -0.5),
        'norm_final': nrm((D_MODEL,), 0.05, 1.0),
    }


def reference(x_prompt, x_sample, state_hgrn, state_rwkv, state_rwkv_shift, cache_swa_k, cache_swa_v,
              norm_mix, w_in, hg_lower_bounds, hg_norm, rw_mu, rw_w0, rw_w2, rw_a0, rw_a2, rw_g2, rw_kk, rw_ka,
              rw_rk, rw_lnw, rw_lnb, sw_sinks, w_branch, w_gate_br, w_out, norm_ffn, ffn_w_gate, ffn_w_up,
              ffn_w_down, moe_router, moe_w_gate, moe_w_up, moe_w_down, norm_final):
    prm = dict(norm_mix=norm_mix, w_in=w_in, hg_lower_bounds=hg_lower_bounds, hg_norm=hg_norm, rw_mu=rw_mu,
               rw_w0=rw_w0, rw_w2=rw_w2, rw_a0=rw_a0, rw_a2=rw_a2, rw_g2=rw_g2, rw_kk=rw_kk, rw_ka=rw_ka,
               rw_rk=rw_rk, rw_lnw=rw_lnw, rw_lnb=rw_lnb, sw_sinks=sw_sinks, w_branch=w_branch,
               w_gate_br=w_gate_br, w_out=w_out, norm_ffn=norm_ffn, ffn_w_gate=ffn_w_gate, ffn_w_up=ffn_w_up,
               ffn_w_down=ffn_w_down, moe_router=moe_router, moe_w_gate=moe_w_gate, moe_w_up=moe_w_up,
               moe_w_down=moe_w_down, norm_final=norm_final)
    B, T = x_prompt.shape[0], x_prompt.shape[1]
    dt = x_prompt.dtype
    hg0 = jnp.zeros((DEPTH, B, HG_HEADS, HG_DK, HG_DV), dt)
    rw0 = jnp.zeros((DEPTH, B, RW_HEADS, RW_HD, RW_HD), dt)
    sh0 = jnp.zeros((DEPTH, B, RW_IN), dt)
    kv0 = jnp.zeros((DEPTH, B, WINDOW, SW_KV_HEADS, SW_HD), dt)
    y_prompt, hg_p, rw_p, sh_p, k_p, v_p = _trunk(x_prompt, jnp.arange(T), hg0, rw0, sh0, kv0, kv0, False, prm)
    Ts = x_sample.shape[1]
    y_sample, hg_s, rw_s, sh_s, k_s, v_s = _trunk(x_sample, PAST_LEN + jnp.arange(Ts), state_hgrn, state_rwkv,
                                                  state_rwkv_shift, cache_swa_k, cache_swa_v, True, prm)
    k_p = k_p[:, :, -WINDOW:]
    v_p = v_p[:, :, -WINDOW:]
    return (y_prompt, y_sample, hg_p, hg_s, rw_p, rw_s, sh_p, sh_s, k_p, v_p, k_s, v_s)
```

```python
import functools

import jax
import jax.numpy as jnp
from jax import lax
from jax.experimental import pallas as pl
from jax.experimental.pallas import tpu as pltpu

F32 = jnp.float32
BF16 = jnp.bfloat16

D_MODEL = 4096
CHUNK = 64
BR_WIDTH = 1024
NORM_EPS = 1e-5
MASK_VALUE = -1e30
HG_HEADS = 8
HG_DK = 128
HG_DV = 128
HG_IN = 4 * BR_WIDTH
RW_HEADS = 16
RW_HD = 64
RW_W_RANK = 64
RW_A_RANK = 64
RW_G_RANK = 128
RW_IN = 3 * BR_WIDTH + RW_W_RANK + RW_A_RANK + RW_G_RANK
RW_LN_EPS = 64e-5
SW_HEADS = 16
SW_KV_HEADS = 2
Q_PER_KV = SW_HEADS // SW_KV_HEADS
SW_HD = 64
WINDOW = 128
ROPE_THETA = 10000.0
SW_IN = SW_HEADS * SW_HD + 2 * SW_KV_HEADS * SW_HD
D_FF = 14336
N_EXPERTS = 8
TOP_K = 2
PAST_LEN = 4096

V7X_VMEM_LIMIT_BYTES = 56 * 1024 * 1024
LANES = 128

HG_BLOCK = 16
RW_CHUNK = 64
RW_HEAD_GROUP = 4
MOE_ROWS = 512


def _cparams(semantics):
    return pltpu.CompilerParams(dimension_semantics=semantics, vmem_limit_bytes=V7X_VMEM_LIMIT_BYTES)


def _sigmoid(x):
    return 1.0 / (1.0 + jnp.exp(-x))


def _split2(x):
    hi = x.astype(BF16)
    lo = (x - hi.astype(F32)).astype(BF16)
    return hi, lo


def _dot3(a, b, dims):
    ah, al = _split2(a)
    bh, bl = _split2(b)
    dg = lambda x, y: lax.dot_general(x, y, (dims, ((), ())), preferred_element_type=F32)
    return dg(ah, bh) + (dg(ah, bl) + dg(al, bh))


_NN = ((1,), (0,))
_NT = ((1,), (1,))
_TN = ((0,), (0,))


def _cumsum_rows(tri, x):
    hi = x.astype(BF16)
    r1 = x - hi.astype(F32)
    mid = r1.astype(BF16)
    lo = (r1 - mid.astype(F32)).astype(BF16)
    d = lambda y: jnp.dot(tri, y, preferred_element_type=F32)
    return d(hi) + (d(mid) + d(lo))


def _rmsnorm_kernel(x_ref, g_ref, o_ref):
    x = x_ref[...]
    y = x * lax.rsqrt(jnp.mean(x * x, axis=-1, keepdims=True) + NORM_EPS)
    o_ref[...] = (y * g_ref[...]).astype(o_ref.dtype)


def _rmsnorm(x, g, out_dtype, tm):
    m, d = x.shape
    return pl.pallas_call(
        _rmsnorm_kernel,
        out_shape=jax.ShapeDtypeStruct((m, d), out_dtype),
        grid=(m // tm,),
        in_specs=[pl.BlockSpec((tm, d), lambda i: (i, 0)), pl.BlockSpec((1, d), lambda i: (0, 0))],
        out_specs=pl.BlockSpec((tm, d), lambda i: (i, 0)),
        compiler_params=_cparams(("parallel",)),
        name="rmsnorm",
    )(x, g.reshape(1, d))


def _mm_kernel(x_ref, w_ref, *rest, nk, has_res):
    o_ref = rest[-1]
    part = jnp.dot(x_ref[...], w_ref[...], preferred_element_type=F32)
    if nk == 1:
        if has_res:
            part = part + rest[0][...]
        o_ref[...] = part.astype(o_ref.dtype)
        return
    k = pl.program_id(2)

    @pl.when(k == 0)
    def _():
        o_ref[...] = part + rest[0][...] if has_res else part

    @pl.when(k > 0)
    def _():
        o_ref[...] += part


def _matmul(x, w, *, tm, tn, tk=None, res=None, out_dtype=F32, name="matmul"):
    m, kd = x.shape
    n = w.shape[1]
    tk = kd if tk is None else tk
    nk = kd // tk
    assert m % tm == 0 and n % tn == 0 and kd % tk == 0
    assert nk == 1 or out_dtype == F32
    in_specs = [pl.BlockSpec((tm, tk), lambda i, j, k: (i, k)), pl.BlockSpec((tk, tn), lambda i, j, k: (k, j))]
    args = [x, w]
    if res is not None:
        in_specs.append(pl.BlockSpec((tm, tn), lambda i, j, k: (i, j)))
        args.append(res)
    return pl.pallas_call(
        functools.partial(_mm_kernel, nk=nk, has_res=res is not None),
        out_shape=jax.ShapeDtypeStruct((m, n), out_dtype),
        grid=(m // tm, n // tn, nk),
        in_specs=in_specs,
        out_specs=pl.BlockSpec((tm, tn), lambda i, j, k: (i, j)),
        compiler_params=_cparams(("parallel", "parallel", "arbitrary")),
        name=name,
    )(*args)


def _glu_kernel(x_ref, wg_ref, wu_ref, o_ref):
    x = x_ref[...]
    a = jnp.dot(x, wg_ref[...], preferred_element_type=F32)
    b = jnp.dot(x, wu_ref[...], preferred_element_type=F32)
    o_ref[...] = ((a * _sigmoid(a)) * b).astype(o_ref.dtype)


def _glu(x, wg, wu, *, tm, tn):
    m, kd = x.shape
    n = wg.shape[1]
    return pl.pallas_call(
        _glu_kernel,
        out_shape=jax.ShapeDtypeStruct((m, n), BF16),
        grid=(m // tm, n // tn),
        in_specs=[pl.BlockSpec((tm, kd), lambda i, j: (i, 0)),
                  pl.BlockSpec((kd, tn), lambda i, j: (0, j)),
                  pl.BlockSpec((kd, tn), lambda i, j: (0, j))],
        out_specs=pl.BlockSpec((tm, tn), lambda i, j: (i, j)),
        compiler_params=_cparams(("parallel", "parallel")),
        name="ffn_glu",
    )(x, wg, wu)


def _gate_merge_kernel(h_ref, g0_ref, g1_ref, g2_ref, b0_ref, b1_ref, b2_ref, wb_ref, o_ref):
    h = h_ref[...]
    acc = None
    for n, (g_ref, b_ref) in enumerate(((g0_ref, b0_ref), (g1_ref, b1_ref), (g2_ref, b2_ref))):
        gate = _sigmoid(jnp.dot(h, g_ref[...], preferred_element_type=F32))
        term = gate * jnp.dot(b_ref[...], wb_ref[n], preferred_element_type=F32)
        acc = term if acc is None else acc + term
    o_ref[...] = acc.astype(o_ref.dtype)


def _gate_merge(h, w_gate, branches, w_branch, *, tm, tn):
    m, d = h.shape
    nj = d // tn
    w = branches[0].shape[1]
    gate_spec = lambda n: pl.BlockSpec((d, tn), lambda i, j, n=n: (0, n * nj + j))
    br_spec = pl.BlockSpec((tm, w), lambda i, j: (i, 0))
    return pl.pallas_call(
        _gate_merge_kernel,
        out_shape=jax.ShapeDtypeStruct((m, d), BF16),
        grid=(m // tm, nj),
        in_specs=[pl.BlockSpec((tm, d), lambda i, j: (i, 0)), gate_spec(0), gate_spec(1), gate_spec(2),
                  br_spec, br_spec, br_spec, pl.BlockSpec((3, w, tn), lambda i, j: (0, 0, j))],
        out_specs=pl.BlockSpec((tm, tn), lambda i, j: (i, j)),
        compiler_params=_cparams(("parallel", "parallel")),
        name="gate_merge",
    )(h, w_gate, w_gate, w_gate, branches[0], branches[1], branches[2], w_branch)


def _hgrn_kernel(q_ref, f_ref, i_ref, g_ref, lb_ref, nw_ref, s0_ref, y_ref, so_ref, st_scr, *, blk, nblk):
    tc = pl.program_id(2)

    @pl.when(tc == 0)
    def _():
        st_scr[...] = s0_ref[0, 0].T

    row = lax.broadcasted_iota(jnp.int32, (blk, blk), 0)
    col = lax.broadcasted_iota(jnp.int32, (blk, blk), 1)
    causal = row >= col
    tri = jnp.where(causal, 1.0, 0.0).astype(BF16)
    lb = lb_ref[...]
    nw = nw_ref[...]

    def body(j, carry):
        sl = pl.ds(pl.multiple_of(j * blk, blk), blk)
        q = q_ref[sl, :]
        f = lb + (1.0 - lb) * _sigmoid(f_ref[sl, :])
        log_f = jnp.log(f)
        k = 1.0 - f
        iv = i_ref[sl, :]
        g = g_ref[sl, :]
        b = _cumsum_rows(tri, log_f)
        qe = (q * _sigmoid(q)) * jnp.exp(b)
        ke = k * jnp.exp(-b)
        st = st_scr[...]
        o = _dot3(qe, st, _NT)
        att = jnp.where(causal, _dot3(qe, ke, _NT), 0.0)
        o = o + _dot3(att, iv, _NN)
        b_last = b[blk - 1:blk, :]
        k_end = k * jnp.exp(b_last - b)
        st_scr[...] = st * jnp.exp(b_last) + _dot3(iv, k_end, _TN)
        y = o * lax.rsqrt(jnp.mean(o * o, axis=-1, keepdims=True) + NORM_EPS) * nw
        y_ref[sl, :] = (y * (g * _sigmoid(g))).astype(y_ref.dtype)
        return carry

    lax.fori_loop(0, nblk, body, 0)

    @pl.when(tc == pl.num_programs(2) - 1)
    def _():
        so_ref[0, 0] = st_scr[...].T


def _hgrn(p_hg, lb, norm_w, s0, *, batch, seq, row_off, tc):
    nt = seq // tc
    rb = row_off // tc
    sec = BR_WIDTH // HG_DV
    spec = lambda s: pl.BlockSpec((tc, HG_DV), lambda b, h, t, s=s: (rb + b * nt + t, s * sec + h))
    vec = pl.BlockSpec((1, HG_DV), lambda b, h, t: (0, h))
    st_spec = pl.BlockSpec((1, 1, HG_DK, HG_DV), lambda b, h, t: (b, h, 0, 0))
    return pl.pallas_call(
        functools.partial(_hgrn_kernel, blk=HG_BLOCK, nblk=tc // HG_BLOCK),
        out_shape=(jax.ShapeDtypeStruct((batch * seq, BR_WIDTH), BF16),
                   jax.ShapeDtypeStruct((batch, HG_HEADS, HG_DK, HG_DV), F32)),
        grid=(batch, HG_HEADS, nt),
        in_specs=[spec(0), spec(1), spec(2), spec(3), vec, vec, st_spec],
        out_specs=(pl.BlockSpec((tc, HG_DV), lambda b, h, t: (b * nt + t, h)), st_spec),
        scratch_shapes=[pltpu.VMEM((HG_DV, HG_DK), F32)],
        compiler_params=_cparams(("parallel", "parallel", "arbitrary")),
        name="hgrn2",
    )(p_hg, p_hg, p_hg, p_hg, lb.reshape(1, BR_WIDTH), norm_w.reshape(1, BR_WIDTH), s0)


def _rwkv_kernel(r_ref, k_ref, v_ref, lo_ref, pr_ref, pk_ref, pv_ref, plo_ref, mur_ref, muk_ref, muv_ref, mulo_ref,
                 w0_ref, a0_ref, kk_ref, ka_ref, rk_ref, lnw_ref, lnb_ref, w2_ref, a2_ref, g2_ref, s0_ref,
                 y_ref, so_ref, s_scr, prev_r, prev_k, prev_v, prev_lo, *, chunk, nsub, heads):
    tc = pl.program_id(2)
    hd = RW_HD

    @pl.when(tc == 0)
    def _():
        s_scr[...] = s0_ref[0]
        prev_r[...] = pr_ref[0]
        prev_k[...] = pk_ref[0]
        prev_v[...] = pv_ref[0]
        prev_lo[...] = plo_ref[0]

    rows = chunk * nsub

    def token_shift(x_ref, prev_scr, mu_ref):
        x = x_ref[...]
        rolled = pltpu.roll(x, 1, 0)
        first = lax.broadcasted_iota(jnp.int32, x.shape, 0) == 0
        shifted = jnp.where(first, prev_scr[...], rolled)
        prev_scr[...] = x[rows - 1:rows, :]
        return x + (shifted - x) * mu_ref[...]

    r = token_shift(r_ref, prev_r, mur_ref)
    k = token_shift(k_ref, prev_k, muk_ref)
    v = token_shift(v_ref, prev_v, muv_ref)
    lo = token_shift(lo_ref, prev_lo, mulo_ref)
    wd = lo[:, :RW_W_RANK]
    ad = lo[:, RW_W_RANK:RW_W_RANK + RW_A_RANK]
    gd = lo[:, RW_W_RANK + RW_A_RANK:]

    wl = w0_ref[...] + _dot3(jnp.tanh(wd), w2_ref[...], _NN)
    neg = -wl
    softplus = jnp.maximum(neg, 0.0) + jnp.log(1.0 + jnp.exp(-jnp.abs(neg)))
    log_decay = -jnp.exp(-softplus - 0.5)
    lr = _sigmoid(a0_ref[...] + _dot3(ad, a2_ref[...], _NN))
    gate = _dot3(_sigmoid(gd), g2_ref[...], _NN)
    kk_raw = k * kk_ref[...]
    k_mod = k * (1.0 + (lr - 1.0) * ka_ref[...])
    rk_w = r * k_mod * rk_ref[...]

    row = lax.broadcasted_iota(jnp.int32, (chunk, chunk), 0)
    col = lax.broadcasted_iota(jnp.int32, (chunk, chunk), 1)
    lower = row >= col
    strict = row > col
    tri = jnp.where(lower, 1.0, 0.0).astype(BF16)
    eye = jnp.where(row == col, 1.0, 0.0)
    row2 = lax.broadcasted_iota(jnp.int32, (chunk, 2 * chunk), 0)
    col2 = lax.broadcasted_iota(jnp.int32, (chunk, 2 * chunk), 1) & (chunk - 1)
    lower2 = row2 >= col2
    strict2 = row2 > col2

    for s in range(nsub):
        rs = slice(s * chunk, (s + 1) * chunk)
        c = _cumsum_rows(tri, log_decay[rs, :])
        c_prev = c - log_decay[rs, :]
        c_last = c[chunk - 1:chunk, :]
        e_c = jnp.exp(c)
        e_cprev = jnp.exp(c_prev)
        e_inv = jnp.exp(-c)
        e_end = jnp.exp(c_last - c)
        e_last = jnp.exp(c_last)
        outs = []
        for j in range(heads):
            cs = slice(j * hd, (j + 1) * hd)
            kkj = kk_raw[rs, cs]
            kkj = kkj / jnp.maximum(jnp.sqrt(jnp.sum(kkj * kkj, axis=-1, keepdims=True)), 1e-12)
            bj = kkj * lr[rs, cs]
            kj = k_mod[rs, cs]
            vj = v[rs, cs]
            rj = r[rs, cs]
            a_dec = -kkj * e_cprev[:, cs]
            r_dec = rj * e_c[:, cs]
            ar = jnp.concatenate([a_dec, r_dec], axis=0)
            bk_inv = jnp.concatenate([bj * e_inv[:, cs], kj * e_inv[:, cs]], axis=0)
            pair = _dot3(ar, bk_inv, _NT)
            m_abk = jnp.where(strict2, pair[:chunk, :], 0.0)
            n_rbk = jnp.where(lower2, pair[chunk:, :], 0.0)
            n_ab = m_abk[:, :chunk]
            inv = eye + n_ab
            pw = n_ab
            steps = chunk.bit_length() - 2
            for _ in range(steps):
                pw = _dot3(pw, pw, _NN)
                inv = inv + _dot3(inv, pw, _NN)
            st = s_scr[j]
            ar_s = _dot3(ar, st, _NT)
            u = _dot3(inv, ar_s[:chunk, :] + _dot3(m_abk[:, chunk:], vj, _NN), _NN)
            uv = jnp.concatenate([u, vj], axis=0)
            y = ar_s[chunk:, :] + _dot3(n_rbk, uv, _NN)
            bk_end = jnp.concatenate([bj * e_end[:, cs], kj * e_end[:, cs]], axis=0)
            s_scr[j] = st * e_last[:, cs] + _dot3(uv, bk_end, _TN)
            mean = jnp.mean(y, axis=-1, keepdims=True)
            var = jnp.mean(jnp.square(y - mean), axis=-1, keepdims=True)
            yn = (y - mean) * lax.rsqrt(var + RW_LN_EPS) * lnw_ref[:, cs] + lnb_ref[:, cs]
            yn = yn + jnp.sum(rk_w[rs, cs], axis=-1, keepdims=True) * vj
            outs.append(yn * gate[rs, cs])
        y_ref[rs, :] = jnp.concatenate(outs, axis=1).astype(y_ref.dtype)

    @pl.when(tc == pl.num_programs(2) - 1)
    def _():
        so_ref[0] = s_scr[...]


def _rwkv(p_rw, prev, s0, prm, *, batch, seq, row_off, nsub):
    hg = RW_HEAD_GROUP
    wcols = hg * RW_HD
    rows = RW_CHUNK * nsub
    nt = seq // rows
    rb = row_off // rows
    nsec = BR_WIDTH // wcols
    lo_blk = 3 * nsec
    assert RW_IN - 3 * BR_WIDTH == wcols
    tok = lambda s: pl.BlockSpec((rows, wcols), lambda b, g, t, s=s: (rb + b * nt + t, s * nsec + g))
    tok_lo = pl.BlockSpec((rows, wcols), lambda b, g, t: (rb + b * nt + t, lo_blk))
    prv = lambda s: pl.BlockSpec((1, 1, wcols), lambda b, g, t, s=s: (b, 0, s * nsec + g))
    prv_lo = pl.BlockSpec((1, 1, wcols), lambda b, g, t: (b, 0, lo_blk))
    mu = lambda s: pl.BlockSpec((1, wcols), lambda b, g, t, s=s: (0, s * nsec + g))
    mu_lo = pl.BlockSpec((1, wcols), lambda b, g, t: (0, lo_blk))
    vec = pl.BlockSpec((1, wcols), lambda b, g, t: (0, g))
    lora = lambda rank: pl.BlockSpec((rank, wcols), lambda b, g, t: (0, g))
    st_spec = pl.BlockSpec((1, hg, RW_HD, RW_HD), lambda b, g, t: (b, g, 0, 0))
    prev3 = prev.reshape(batch, 1, RW_IN)
    mu2 = prm["mu"].reshape(1, RW_IN)
    v1 = lambda a: a.reshape(1, BR_WIDTH)
    return pl.pallas_call(
        functools.partial(_rwkv_kernel, chunk=RW_CHUNK, nsub=nsub, heads=hg),
        out_shape=(jax.ShapeDtypeStruct((batch * seq, BR_WIDTH), BF16),
                   jax.ShapeDtypeStruct((batch, RW_HEADS, RW_HD, RW_HD), F32)),
        grid=(batch, RW_HEADS // hg, nt),
        in_specs=[tok(0), tok(1), tok(2), tok_lo, prv(0), prv(1), prv(2), prv_lo, mu(0), mu(1), mu(2), mu_lo,
                  vec, vec, vec, vec, vec, vec, vec, lora(RW_W_RANK), lora(RW_A_RANK), lora(RW_G_RANK), st_spec],
        out_specs=(pl.BlockSpec((rows, wcols), lambda b, g, t: (b * nt + t, g)), st_spec),
        scratch_shapes=[pltpu.VMEM((hg, RW_HD, RW_HD), F32)] + [pltpu.VMEM((1, wcols), F32)] * 4,
        compiler_params=_cparams(("parallel", "parallel", "arbitrary")),
        name="rwkv7",
    )(p_rw, p_rw, p_rw, p_rw, prev3, prev3, prev3, prev3, mu2, mu2, mu2, mu2,
      v1(prm["w0"]), v1(prm["a0"]), v1(prm["kk"]), v1(prm["ka"]), v1(prm["rk"]), v1(prm["lnw"]), v1(prm["lnb"]),
      prm["w2"], prm["a2"], prm["g2"], s0)


def _rope_partner(x):
    width = x.shape[-1]
    half = SW_HD // 2
    lane = lax.broadcasted_iota(jnp.int32, x.shape, x.ndim - 1)
    return jnp.where(lane % SW_HD < half, pltpu.roll(x, width - half, x.ndim - 1), pltpu.roll(x, half, x.ndim - 1))


def _swa_kernel(sink_ref, q_ref, kv_ref, cos_ref, sin_ref, past_ref, o_ref, kvo_ref, buf, *, past_valid):
    c = pl.program_id(1)
    kvw = SW_KV_HEADS * SW_HD

    @pl.when(c == 0)
    def _():
        buf[0:WINDOW, :] = past_ref[0]

    @pl.when(c > 0)
    def _():
        buf[0:WINDOW, :] = buf[CHUNK:CHUNK + WINDOW, :]

    cos = cos_ref[...]
    sin = sin_ref[...]
    kv = kv_ref[...]
    k = kv[:, :kvw]
    v = kv[:, kvw:]
    k_rot = k * cos + _rope_partner(k) * sin
    buf[WINDOW:WINDOW + CHUNK, :kvw] = k_rot
    buf[WINDOW:WINDOW + CHUNK, kvw:] = v
    kvo_ref[:, :kvw] = k_rot
    kvo_ref[:, kvw:] = v

    q = q_ref[...]
    reps = q.shape[1] // cos.shape[1]
    cos_q = jnp.concatenate([cos] * reps, axis=1)
    sin_q = jnp.concatenate([sin] * reps, axis=1)
    q_rot = q * cos_q + _rope_partner(q) * sin_q

    band = WINDOW + CHUNK
    slot = lax.broadcasted_iota(jnp.int32, (Q_PER_KV * CHUNK, band), 1)
    if past_valid:
        valid = None
    else:
        valid = slot >= (WINDOW // CHUNK - c) * CHUNK
    outs = []
    for g in range(SW_KV_HEADS):
        kg = buf[:, g * SW_HD:(g + 1) * SW_HD].astype(BF16)
        vg = buf[:, kvw + g * SW_HD:kvw + (g + 1) * SW_HD].astype(BF16)
        heads = range(g * Q_PER_KV, (g + 1) * Q_PER_KV)
        qg = jnp.concatenate([q_rot[:, h * SW_HD:(h + 1) * SW_HD] for h in heads], axis=0).astype(BF16)
        s = lax.dot_general(qg, kg, (_NT, ((), ())), preferred_element_type=F32) * (SW_HD ** -0.5)
        if valid is not None:
            s = jnp.where(valid, s, MASK_VALUE)
        sink = jnp.concatenate([jnp.full((CHUNK, 1), sink_ref[h], F32) for h in heads], axis=0)
        m = jnp.maximum(jnp.max(s, axis=-1, keepdims=True), sink)
        e = jnp.exp(s - m)
        den = jnp.sum(e, axis=-1, keepdims=True) + jnp.exp(sink - m)
        og = jnp.dot(e.astype(BF16), vg, preferred_element_type=F32) / den
        outs.extend(og[i * CHUNK:(i + 1) * CHUNK, :] for i in range(Q_PER_KV))
    o_ref[...] = jnp.concatenate(outs, axis=1).astype(o_ref.dtype)


def _swa(p_sw, cos_tab, sin_tab, past_kv, sinks, *, batch, seq, row_off, past_valid):
    nc = seq // CHUNK
    rb = row_off // CHUNK
    qw = SW_HEADS * SW_HD
    kvw2 = 2 * SW_KV_HEADS * SW_HD
    grid_spec = pltpu.PrefetchScalarGridSpec(
        num_scalar_prefetch=1,
        grid=(batch, nc),
        in_specs=[pl.BlockSpec((CHUNK, qw), lambda b, c, s: (rb + b * nc + c, 0)),
                  pl.BlockSpec((CHUNK, kvw2), lambda b, c, s: (rb + b * nc + c, qw // kvw2)),
                  pl.BlockSpec((CHUNK, LANES), lambda b, c, s: (c, 0)),
                  pl.BlockSpec((CHUNK, LANES), lambda b, c, s: (c, 0)),
                  pl.BlockSpec((1, WINDOW, kvw2), lambda b, c, s: (b, 0, 0))],
        out_specs=(pl.BlockSpec((CHUNK, qw), lambda b, c, s: (b * nc + c, 0)),
                   pl.BlockSpec((CHUNK, kvw2), lambda b, c, s: (b * nc + c, 0))),
        scratch_shapes=[pltpu.VMEM((WINDOW + CHUNK, kvw2), F32)],
    )
    return pl.pallas_call(
        functools.partial(_swa_kernel, past_valid=past_valid),
        out_shape=(jax.ShapeDtypeStruct((batch * seq, qw), BF16),
                   jax.ShapeDtypeStruct((batch * seq, kvw2), F32)),
        grid_spec=grid_spec,
        compiler_params=_cparams(("parallel", "arbitrary")),
        name="swa",
    )(sinks, p_sw, p_sw, cos_tab, sin_tab, past_kv)


def _rope_tables(pos):
    half = SW_HD // 2
    inv = ROPE_THETA ** (-jnp.arange(half, dtype=F32) / half)
    ang = pos.astype(F32)[:, None] * inv[None, :]
    cos, sin = jnp.cos(ang), jnp.sin(ang)
    reps = LANES // SW_HD
    return jnp.tile(jnp.concatenate([cos, cos], axis=1), (1, reps)), jnp.tile(jnp.concatenate([-sin, sin], axis=1), (1, reps))


def _norm_router_kernel(x_ref, g_ref, wr_ref, h_ref, route_ref):
    x = x_ref[...]
    y = x * lax.rsqrt(jnp.mean(x * x, axis=-1, keepdims=True) + NORM_EPS)
    h = y * g_ref[...]
    h_ref[...] = h.astype(h_ref.dtype)
    logits = _dot3(h, wr_ref[...], _NN)
    lane = lax.broadcasted_iota(jnp.int32, logits.shape, 1).astype(F32)
    neg_inf = -jnp.inf
    logits = jnp.where(lane < N_EXPERTS, logits, neg_inf)
    m1 = jnp.max(logits, axis=-1, keepdims=True)
    i1 = jnp.min(jnp.where(logits == m1, lane, float(LANES)), axis=-1, keepdims=True)
    rest = jnp.where(lane == i1, neg_inf, logits)
    m2 = jnp.max(rest, axis=-1, keepdims=True)
    i2 = jnp.min(jnp.where(rest == m2, lane, float(LANES)), axis=-1, keepdims=True)
    e2 = jnp.exp(m2 - m1)
    g1 = 1.0 / (1.0 + e2)
    g2 = e2 / (1.0 + e2)
    route = jnp.where(lane == 0, i1, jnp.where(lane == 1, i2, jnp.where(lane == 2, g1, jnp.where(lane == 3, g2, 0.0))))
    route_ref[...] = route


def _norm_router(x, g, router, *, tm):
    m, d = x.shape
    wr = jnp.zeros((d, LANES), F32).at[:, :N_EXPERTS].set(router)
    return pl.pallas_call(
        _norm_router_kernel,
        out_shape=(jax.ShapeDtypeStruct((m, d), BF16), jax.ShapeDtypeStruct((m, LANES), F32)),
        grid=(m // tm,),
        in_specs=[pl.BlockSpec((tm, d), lambda i: (i, 0)), pl.BlockSpec((1, d), lambda i: (0, 0)),
                  pl.BlockSpec((d, LANES), lambda i: (0, 0))],
        out_specs=(pl.BlockSpec((tm, d), lambda i: (i, 0)), pl.BlockSpec((tm, LANES), lambda i: (i, 0))),
        compiler_params=_cparams(("parallel",)),
        name="norm_router",
    )(x, g.reshape(1, d), wr)


def _row_copy(src_hbm, dst_hbm, sem, src_row, dst_row):
    return pltpu.make_async_copy(src_hbm.at[pl.ds(src_row, 1)], dst_hbm.at[pl.ds(dst_row, 1)], sem)


def _gather_rows_kernel(idx_ref, src_hbm, dst_hbm, sem, *, rows):
    base = pl.program_id(0) * rows

    def start(r, carry):
        _row_copy(src_hbm, dst_hbm, sem, idx_ref[0, 0, r], base + r).start()
        return carry

    lax.fori_loop(0, rows, start, 0)

    def wait(r, carry):
        _row_copy(src_hbm, dst_hbm, sem, 0, base + r).wait()
        return carry

    lax.fori_loop(0, rows, wait, 0)


def _gather_rows(src, idx, *, rows):
    n = idx.shape[0]
    nb = n // rows
    return pl.pallas_call(
        functools.partial(_gather_rows_kernel, rows=rows),
        out_shape=jax.ShapeDtypeStruct((n, src.shape[1]), src.dtype),
        grid=(nb,),
        in_specs=[pl.BlockSpec((1, 1, rows), lambda i: (i, 0, 0), memory_space=pltpu.SMEM),
                  pl.BlockSpec(memory_space=pl.ANY)],
        out_specs=pl.BlockSpec(memory_space=pl.ANY),
        scratch_shapes=[pltpu.SemaphoreType.DMA(())],
        compiler_params=_cparams(("arbitrary",)),
        name="moe_gather",
    )(idx.reshape(nb, 1, rows), src)


def _moe_glu_kernel(be_ref, nv_ref, x_ref, wg_ref, wu_ref, o_ref):
    used = pl.program_id(1) < nv_ref[0]

    @pl.when(used)
    def _():
        x = x_ref[...]
        a = jnp.dot(x, wg_ref[0], preferred_element_type=F32)
        b = jnp.dot(x, wu_ref[0], preferred_element_type=F32)
        o_ref[...] = ((a * _sigmoid(a)) * b).astype(o_ref.dtype)

    @pl.when(jnp.logical_not(used))
    def _():
        o_ref[...] = jnp.zeros_like(o_ref)


def _moe_glu(rows_x, wg, wu, block_e, n_valid, *, tn):
    n, d = rows_x.shape
    nb = n // MOE_ROWS
    f = wg.shape[2]
    blk = lambda b, nv: jnp.minimum(b, nv[0] - 1)
    grid_spec = pltpu.PrefetchScalarGridSpec(
        num_scalar_prefetch=2,
        grid=(f // tn, nb),
        in_specs=[pl.BlockSpec((MOE_ROWS, d), lambda j, b, be, nv: (blk(b, nv), 0)),
                  pl.BlockSpec((1, d, tn), lambda j, b, be, nv: (be[blk(b, nv)], 0, j)),
                  pl.BlockSpec((1, d, tn), lambda j, b, be, nv: (be[blk(b, nv)], 0, j))],
        out_specs=pl.BlockSpec((MOE_ROWS, tn), lambda j, b, be, nv: (b, j)),
    )
    return pl.pallas_call(
        _moe_glu_kernel,
        out_shape=jax.ShapeDtypeStruct((n, f), BF16),
        grid_spec=grid_spec,
        compiler_params=_cparams(("arbitrary", "arbitrary")),
        name="moe_glu",
    )(block_e, n_valid, rows_x, wg, wu)


def _moe_down_kernel(be_ref, nv_ref, x_ref, w_ref, o_ref):
    k = pl.program_id(2)

    used = pl.program_id(1) < nv_ref[0]

    @pl.when(used)
    def _():
        part = jnp.dot(x_ref[...], w_ref[0], preferred_element_type=F32)

        @pl.when(k == 0)
        def _():
            o_ref[...] = part

        @pl.when(k > 0)
        def _():
            o_ref[...] += part

    @pl.when(jnp.logical_not(used))
    def _():
        o_ref[...] = jnp.zeros_like(o_ref)


def _moe_down(act, wd, block_e, n_valid, *, tn, tk):
    n, f = act.shape
    nb = n // MOE_ROWS
    d = wd.shape[2]
    blk = lambda b, nv: jnp.minimum(b, nv[0] - 1)
    grid_spec = pltpu.PrefetchScalarGridSpec(
        num_scalar_prefetch=2,
        grid=(d // tn, nb, f // tk),
        in_specs=[pl.BlockSpec((MOE_ROWS, tk), lambda j, b, k, be, nv: (blk(b, nv), k)),
                  pl.BlockSpec((1, tk, tn), lambda j, b, k, be, nv: (be[blk(b, nv)], k, j))],
        out_specs=pl.BlockSpec((MOE_ROWS, tn), lambda j, b, k, be, nv: (b, j)),
    )
    return pl.pallas_call(
        _moe_down_kernel,
        out_shape=jax.ShapeDtypeStruct((n, d), F32),
        grid_spec=grid_spec,
        compiler_params=_cparams(("arbitrary", "arbitrary", "arbitrary")),
        name="moe_down",
    )(block_e, n_valid, act, wd)


def _combine_kernel(p0_ref, p1_ref, x_ref, gate_ref, out_hbm, o_ref, buf0, buf1, sem, *, rows):
    def copies(r):
        return (pltpu.make_async_copy(out_hbm.at[pl.ds(p0_ref[0, 0, r], 1)], buf0.at[pl.ds(r, 1)], sem.at[0]),
                pltpu.make_async_copy(out_hbm.at[pl.ds(p1_ref[0, 0, r], 1)], buf1.at[pl.ds(r, 1)], sem.at[1]))

    def start(r, carry):
        c0, c1 = copies(r)
        c0.start()
        c1.start()
        return carry

    lax.fori_loop(0, rows, start, 0)

    def wait(r, carry):
        c0, c1 = copies(r)
        c0.wait()
        c1.wait()
        return carry

    lax.fori_loop(0, rows, wait, 0)
    gate = gate_ref[...]
    g0 = gate[:, 2:3]
    g1 = gate[:, 3:4]
    o_ref[...] = x_ref[...] + (buf0[...] * g0 + buf1[...] * g1)


def _combine(x, route, out_rows, pos0, pos1, *, rows):
    m, d = x.shape
    nb = m // rows
    idx_spec = pl.BlockSpec((1, 1, rows), lambda i: (i, 0, 0), memory_space=pltpu.SMEM)
    return pl.pallas_call(
        functools.partial(_combine_kernel, rows=rows),
        out_shape=jax.ShapeDtypeStruct((m, d), F32),
        grid=(nb,),
        in_specs=[idx_spec, idx_spec, pl.BlockSpec((rows, d), lambda i: (i, 0)),
                  pl.BlockSpec((rows, LANES), lambda i: (i, 0)), pl.BlockSpec(memory_space=pl.ANY)],
        out_specs=pl.BlockSpec((rows, d), lambda i: (i, 0)),
        scratch_shapes=[pltpu.VMEM((rows, d), F32), pltpu.VMEM((rows, d), F32), pltpu.SemaphoreType.DMA((2,))],
        compiler_params=_cparams(("arbitrary",)),
        name="moe_combine",
    )(pos0.reshape(nb, 1, rows), pos1.reshape(nb, 1, rows), x, route, out_rows)


def _moe_layer(x, g, router, wg, wu, wd, *, tm, glu_tn, down_tn, down_tk):
    m, d = x.shape
    h, route = _norm_router(x, g, router, tm=tm)
    eid = route[:, :TOP_K].astype(jnp.int32).reshape(-1)
    n_assign = m * TOP_K
    onehot = (eid[:, None] == jnp.arange(N_EXPERTS)[None, :]).astype(jnp.int32)
    rank = jnp.take_along_axis(jnp.cumsum(onehot, axis=0), eid[:, None], axis=1)[:, 0] - 1
    counts = jnp.sum(onehot, axis=0)
    padded = (counts + MOE_ROWS - 1) // MOE_ROWS * MOE_ROWS
    pend = jnp.cumsum(padded)
    pstart = pend - padded
    dest = pstart[eid] + rank
    nb = -(-n_assign // MOE_ROWS) + N_EXPERTS
    tok = jnp.arange(n_assign, dtype=jnp.int32) // TOP_K
    src_tok = jnp.zeros((nb * MOE_ROWS,), jnp.int32).at[dest].set(tok)
    block_e = jnp.minimum(jnp.searchsorted(pend, jnp.arange(nb) * MOE_ROWS, side="right"), N_EXPERTS - 1)
    block_e = block_e.astype(jnp.int32)
    n_valid = (pend[-1] // MOE_ROWS).astype(jnp.int32).reshape(1)
    h_words = lax.bitcast_convert_type(h.reshape(m, d // 2, 2), jnp.uint32)
    rows_words = _gather_rows(h_words, src_tok, rows=MOE_ROWS)
    rows_x = lax.bitcast_convert_type(rows_words, BF16).reshape(nb * MOE_ROWS, d)
    act = _moe_glu(rows_x, wg, wu, block_e, n_valid, tn=glu_tn)
    out_rows = _moe_down(act, wd, block_e, n_valid, tn=down_tn, tk=down_tk)
    dest2 = dest.reshape(m, TOP_K).astype(jnp.int32)
    return _combine(x, route, out_rows, dest2[:, 0], dest2[:, 1], rows=256)


def kernel(x_prompt, x_sample, state_hgrn, state_rwkv, state_rwkv_shift, cache_swa_k, cache_swa_v, norm_mix, w_in, hg_lower_bounds, hg_norm, rw_mu, rw_w0, rw_w2, rw_a0, rw_a2, rw_g2, rw_kk, rw_ka, rw_rk, rw_lnw, rw_lnb, sw_sinks, w_branch, w_gate_br, w_out, norm_ffn, ffn_w_gate, ffn_w_up, ffn_w_down, moe_router, moe_w_gate, moe_w_up, moe_w_down, norm_final):
    bp, tp, d = x_prompt.shape
    bs, ts, _ = x_sample.shape
    mp, ms = bp * tp, bs * ts
    m = mp + ms
    depth = w_in.shape[0]
    tm = 768
    tm_norm = 256
    x = jnp.concatenate([x_prompt.reshape(mp, d), x_sample.reshape(ms, d)], axis=0)

    lbs = jax.nn.softmax(hg_lower_bounds.astype(F32), axis=0)
    lbs = jnp.cumsum(lbs, axis=0) - lbs[0]
    cos_p, sin_p = _rope_tables(jnp.arange(tp))
    cos_s, sin_s = _rope_tables(PAST_LEN + jnp.arange(ts))
    kvw = SW_KV_HEADS * SW_HD

    outs = {k: [] for k in ("hg_p", "hg_s", "rw_p", "rw_s", "sh_p", "sh_s", "k_p", "v_p", "k_s", "v_s")}
    for l in range(depth):
        h = _rmsnorm(x, norm_mix[l], BF16, tm_norm)
        w_in_l = w_in[l]
        p_hg = _matmul(h, w_in_l[:, :HG_IN].astype(BF16), tm=tm, tn=512, name="proj_hg")
        p_rw = _matmul(h, w_in_l[:, HG_IN:HG_IN + RW_IN].astype(BF16), tm=tm, tn=256, name="proj_rw")
        p_sw = _matmul(h, w_in_l[:, HG_IN + RW_IN:].astype(BF16), tm=tm, tn=256, name="proj_sw")

        y_hg_p, s_hg_p = _hgrn(p_hg, lbs[l], hg_norm[l], jnp.zeros((bp, HG_HEADS, HG_DK, HG_DV), F32),
                               batch=bp, seq=tp, row_off=0, tc=256)
        y_hg_s, s_hg_s = _hgrn(p_hg, lbs[l], hg_norm[l], state_hgrn[l], batch=bs, seq=ts, row_off=mp, tc=ts)
        rw_prm = dict(mu=rw_mu[l], w0=rw_w0[l], w2=rw_w2[l], a0=rw_a0[l], a2=rw_a2[l], g2=rw_g2[l], kk=rw_kk[l],
                      ka=rw_ka[l], rk=rw_rk[l], lnw=rw_lnw[l], lnb=rw_lnb[l])
        y_rw_p, s_rw_p = _rwkv(p_rw, jnp.zeros((bp, RW_IN), F32), jnp.zeros((bp, RW_HEADS, RW_HD, RW_HD), F32),
                               rw_prm, batch=bp, seq=tp, row_off=0, nsub=2)
        y_rw_s, s_rw_s = _rwkv(p_rw, state_rwkv_shift[l], state_rwkv[l], rw_prm, batch=bs, seq=ts, row_off=mp, nsub=1)
        past_s = jnp.concatenate([cache_swa_k[l].reshape(bs, WINDOW, kvw), cache_swa_v[l].reshape(bs, WINDOW, kvw)],
                                 axis=-1)
        y_sw_p, kv_p = _swa(p_sw, cos_p, sin_p, jnp.zeros((bp, WINDOW, 2 * kvw), F32), sw_sinks[l],
                            batch=bp, seq=tp, row_off=0, past_valid=False)
        y_sw_s, kv_s = _swa(p_sw, cos_s, sin_s, past_s, sw_sinks[l], batch=bs, seq=ts, row_off=mp, past_valid=True)

        branches = [jnp.concatenate([a, b], axis=0) for a, b in ((y_hg_p, y_hg_s), (y_rw_p, y_rw_s), (y_sw_p, y_sw_s))]
        merged = _gate_merge(h, w_gate_br[l].astype(BF16), branches, w_branch[l].astype(BF16), tm=tm, tn=256)
        x = _matmul(merged, w_out[l].astype(BF16), tm=tm, tn=512, res=x, name="w_out")

        j = l // 2
        if l % 2 == 0:
            h2 = _rmsnorm(x, norm_ffn[l], BF16, tm_norm)
            act = _glu(h2, ffn_w_gate[j].astype(BF16), ffn_w_up[j].astype(BF16), tm=tm, tn=512)
            x = _matmul(act, ffn_w_down[j].astype(BF16), tm=tm, tn=1024, tk=2048, res=x, name="ffn_down")
        else:
            x = _moe_layer(x, norm_ffn[l], moe_router[j], moe_w_gate[j].astype(BF16), moe_w_up[j].astype(BF16),
                           moe_w_down[j].astype(BF16), tm=tm_norm, glu_tn=1024, down_tn=2048, down_tk=2048)

        outs["hg_p"].append(s_hg_p)
        outs["hg_s"].append(s_hg_s)
        outs["rw_p"].append(s_rw_p)
        outs["rw_s"].append(s_rw_s)
        p_rw_p = p_rw[:mp].reshape(bp, tp, RW_IN)
        p_rw_s = p_rw[mp:].reshape(bs, ts, RW_IN)
        outs["sh_p"].append(p_rw_p[:, -1])
        outs["sh_s"].append(p_rw_s[:, -1])
        kv_p = kv_p.reshape(bp, tp, 2, SW_KV_HEADS, SW_HD)[:, -WINDOW:]
        kv_s = kv_s.reshape(bs, ts, 2, SW_KV_HEADS, SW_HD)
        outs["k_p"].append(kv_p[:, :, 0])
        outs["v_p"].append(kv_p[:, :, 1])
        outs["k_s"].append(kv_s[:, :, 0])
        outs["v_s"].append(kv_s[:, :, 1])

    y = _rmsnorm(x, norm_final, F32, tm_norm)
    st = {k: jnp.stack(v) for k, v in outs.items()}
    return (y[:mp].reshape(bp, tp, d), y[mp:].reshape(bs, ts, d), st["hg_p"], st["hg_s"], st["rw_p"], st["rw_s"],
            st["sh_p"], st["sh_s"], st["k_p"], st["v_p"], st["k_s"], st["v_s"])
```

```python
import functools

import jax
import jax.numpy as jnp
from jax import lax
from jax.experimental import pallas as pl
from jax.experimental.pallas import tpu as pltpu

F32 = jnp.float32
BF16 = jnp.bfloat16

D_MODEL = 4096
CHUNK = 64
BR_WIDTH = 1024
NORM_EPS = 1e-5
MASK_VALUE = -1e30
HG_HEADS = 8
HG_DK = 128
HG_DV = 128
HG_IN = 4 * BR_WIDTH
RW_HEADS = 16
RW_HD = 64
RW_W_RANK = 64
RW_A_RANK = 64
RW_G_RANK = 128
RW_IN = 3 * BR_WIDTH + RW_W_RANK + RW_A_RANK + RW_G_RANK
RW_LN_EPS = 64e-5
SW_HEADS = 16
SW_KV_HEADS = 2
Q_PER_KV = SW_HEADS // SW_KV_HEADS
SW_HD = 64
WINDOW = 128
ROPE_THETA = 10000.0
SW_IN = SW_HEADS * SW_HD + 2 * SW_KV_HEADS * SW_HD
D_FF = 14336
N_EXPERTS = 8
TOP_K = 2
PAST_LEN = 4096

V7X_VMEM_LIMIT_BYTES = 56 * 1024 * 1024
LANES = 128

HG_BLOCK = 16
RW_CHUNK = 64
RW_HEAD_GROUP = 4
MOE_ROWS = 512


def _cparams(semantics):
    return pltpu.CompilerParams(dimension_semantics=semantics, vmem_limit_bytes=V7X_VMEM_LIMIT_BYTES)


def _sigmoid(x):
    return 1.0 / (1.0 + jnp.exp(-x))


def _split2(x):
    hi = x.astype(BF16)
    lo = (x - hi.astype(F32)).astype(BF16)
    return hi, lo


def _dot1(a, b, dims):
    return lax.dot_general(a.astype(BF16), b.astype(BF16), (dims, ((), ())), preferred_element_type=F32)


def _dot3(a, b, dims):
    ah, al = _split2(a)
    bh, bl = _split2(b)
    dg = lambda x, y: lax.dot_general(x, y, (dims, ((), ())), preferred_element_type=F32)
    return dg(ah, bh) + (dg(ah, bl) + dg(al, bh))


_NN = ((1,), (0,))
_NT = ((1,), (1,))
_TN = ((0,), (0,))


def _split3(x):
    hi = x.astype(BF16)
    r1 = x - hi.astype(F32)
    mid = r1.astype(BF16)
    lo = (r1 - mid.astype(F32)).astype(BF16)
    return hi, mid, lo


def _sum_dots(mask, pieces):
    hi, mid, lo = pieces
    d = lambda y: jnp.dot(mask, y, preferred_element_type=F32)
    return d(hi) + (d(mid) + d(lo))


def _cumsum_rows(tri, x):
    return _sum_dots(tri, _split3(x))


def _rmsnorm_kernel(x_ref, g_ref, o_ref):
    x = x_ref[...]
    y = x * lax.rsqrt(jnp.mean(x * x, axis=-1, keepdims=True) + NORM_EPS)
    o_ref[...] = (y * g_ref[...]).astype(o_ref.dtype)


def _rmsnorm(x, g, out_dtype, tm, row_off=0, nrows=None):
    d = x.shape[1]
    m = x.shape[0] if nrows is None else nrows
    rb = row_off // tm
    return pl.pallas_call(
        _rmsnorm_kernel,
        out_shape=jax.ShapeDtypeStruct((m, d), out_dtype),
        grid=(m // tm,),
        in_specs=[pl.BlockSpec((tm, d), lambda i: (rb + i, 0)), pl.BlockSpec((1, d), lambda i: (0, 0))],
        out_specs=pl.BlockSpec((tm, d), lambda i: (i, 0)),
        compiler_params=_cparams(("parallel",)),
        name="rmsnorm",
    )(x, g.reshape(1, d))


def _mm_kernel(x_ref, w_ref, *rest, nk, has_res):
    o_ref = rest[-1]
    part = jnp.dot(x_ref[...], w_ref[...], preferred_element_type=F32)
    if nk == 1:
        if has_res:
            part = part + rest[0][...]
        o_ref[...] = part.astype(o_ref.dtype)
        return
    k = pl.program_id(2)

    @pl.when(k == 0)
    def _():
        o_ref[...] = part + rest[0][...] if has_res else part

    @pl.when(k > 0)
    def _():
        o_ref[...] += part


def _matmul(x, w, *, tm, tn, tk=None, res=None, out_dtype=F32, name="matmul"):
    m, kd = x.shape
    n = w.shape[1]
    tk = kd if tk is None else tk
    nk = kd // tk
    assert m % tm == 0 and n % tn == 0 and kd % tk == 0
    assert nk == 1 or out_dtype == F32
    in_specs = [pl.BlockSpec((tm, tk), lambda i, j, k: (i, k)), pl.BlockSpec((tk, tn), lambda i, j, k: (k, j))]
    args = [x, w]
    if res is not None:
        in_specs.append(pl.BlockSpec((tm, tn), lambda i, j, k: (i, j)))
        args.append(res)
    return pl.pallas_call(
        functools.partial(_mm_kernel, nk=nk, has_res=res is not None),
        out_shape=jax.ShapeDtypeStruct((m, n), out_dtype),
        grid=(m // tm, n // tn, nk),
        in_specs=in_specs,
        out_specs=pl.BlockSpec((tm, tn), lambda i, j, k: (i, j)),
        compiler_params=_cparams(("parallel", "parallel", "arbitrary")),
        name=name,
    )(*args)


def _glu_kernel(x_ref, wg_ref, wu_ref, o_ref):
    x = x_ref[...]
    a = jnp.dot(x, wg_ref[...], preferred_element_type=F32)
    b = jnp.dot(x, wu_ref[...], preferred_element_type=F32)
    o_ref[...] = ((a * _sigmoid(a)) * b).astype(o_ref.dtype)


def _glu(x, wg, wu, *, tm, tn):
    m, kd = x.shape
    n = wg.shape[1]
    return pl.pallas_call(
        _glu_kernel,
        out_shape=jax.ShapeDtypeStruct((m, n), BF16),
        grid=(m // tm, n // tn),
        in_specs=[pl.BlockSpec((tm, kd), lambda i, j: (i, 0)),
                  pl.BlockSpec((kd, tn), lambda i, j: (0, j)),
                  pl.BlockSpec((kd, tn), lambda i, j: (0, j))],
        out_specs=pl.BlockSpec((tm, tn), lambda i, j: (i, j)),
        compiler_params=_cparams(("parallel", "parallel")),
        name="ffn_glu",
    )(x, wg, wu)


def _gate_merge_kernel(h_ref, g0_ref, g1_ref, g2_ref, b0_ref, b1_ref, b2_ref, wb_ref, o_ref):
    h = h_ref[...]
    acc = None
    for n, (g_ref, b_ref) in enumerate(((g0_ref, b0_ref), (g1_ref, b1_ref), (g2_ref, b2_ref))):
        gate = _sigmoid(jnp.dot(h, g_ref[...], preferred_element_type=F32))
        term = gate * jnp.dot(b_ref[...], wb_ref[n], preferred_element_type=F32)
        acc = term if acc is None else acc + term
    o_ref[...] = acc.astype(o_ref.dtype)


def _gate_merge(h, w_gate, branches, w_branch, *, tm, tn):
    m, d = h.shape
    nj = d // tn
    w = branches[0].shape[1]
    gate_spec = lambda n: pl.BlockSpec((d, tn), lambda i, j, n=n: (0, n * nj + j))
    br_spec = pl.BlockSpec((tm, w), lambda i, j: (i, 0))
    return pl.pallas_call(
        _gate_merge_kernel,
        out_shape=jax.ShapeDtypeStruct((m, d), BF16),
        grid=(m // tm, nj),
        in_specs=[pl.BlockSpec((tm, d), lambda i, j: (i, 0)), gate_spec(0), gate_spec(1), gate_spec(2),
                  br_spec, br_spec, br_spec, pl.BlockSpec((3, w, tn), lambda i, j: (0, 0, j))],
        out_specs=pl.BlockSpec((tm, tn), lambda i, j: (i, j)),
        compiler_params=_cparams(("parallel", "parallel")),
        name="gate_merge",
    )(h, w_gate, w_gate, w_gate, branches[0], branches[1], branches[2], w_branch)


def _hgrn_kernel(q_ref, f_ref, i_ref, g_ref, lb_ref, nw_ref, s0_ref, y_ref, so_ref, st_scr, *, blk, nblk):
    tc = pl.program_id(2)

    @pl.when(tc == 0)
    def _():
        st_scr[...] = s0_ref[0, 0].T

    rows = blk * nblk
    row = lax.broadcasted_iota(jnp.int32, (rows, rows), 0)
    col = lax.broadcasted_iota(jnp.int32, (rows, rows), 1)
    in_block = (row - col).astype(jnp.uint32) <= (row & (blk - 1)).astype(jnp.uint32)
    same_block = (row & -blk) == (col & -blk)
    tri = jnp.where(in_block, 1.0, 0.0).astype(BF16)
    ones_bd = jnp.where(same_block, 1.0, 0.0).astype(BF16)
    lb = lb_ref[...]
    q = q_ref[...]
    f = lb + (1.0 - lb) * _sigmoid(f_ref[...])
    log_f = jnp.log(f)
    k = 1.0 - f
    iv = i_ref[...]
    g = g_ref[...]
    pieces = _split3(log_f)
    b = _sum_dots(tri, pieces)
    total = _sum_dots(ones_bd, pieces)
    qe = (q * _sigmoid(q)) * jnp.exp(b)
    ke = k * jnp.exp(-b)
    k_end = k * jnp.exp(total - b)
    decay = jnp.exp(total)
    att = jnp.where(in_block, _dot1(qe, ke, _NT), 0.0)
    o_intra = _dot1(att, iv, _NN)
    blocks = [slice(n * blk, (n + 1) * blk) for n in range(nblk)]
    grams = [_dot1(iv[sl, :], k_end[sl, :], _TN) for sl in blocks]
    states = [st_scr[...]]
    for n, sl in enumerate(blocks):
        states.append(states[n] * decay[sl.start:sl.start + 1, :] + grams[n])
    st_scr[...] = states[nblk]
    o_inter = [_dot1(qe[sl, :], states[n], _NT) for n, sl in enumerate(blocks)]
    o = o_intra + jnp.concatenate(o_inter, axis=0)
    y = o * lax.rsqrt(jnp.mean(o * o, axis=-1, keepdims=True) + NORM_EPS) * nw_ref[...]
    y_ref[...] = (y * (g * _sigmoid(g))).astype(y_ref.dtype)

    @pl.when(tc == pl.num_programs(2) - 1)
    def _():
        so_ref[0, 0] = st_scr[...].T


def _hgrn(p_hg, lb, norm_w, s0, *, batch, seq, row_off, tc):
    nt = seq // tc
    rb = row_off // tc
    sec = BR_WIDTH // HG_DV
    spec = lambda s: pl.BlockSpec((tc, HG_DV), lambda b, h, t, s=s: (rb + b * nt + t, s * sec + h))
    vec = pl.BlockSpec((1, HG_DV), lambda b, h, t: (0, h))
    st_spec = pl.BlockSpec((1, 1, HG_DK, HG_DV), lambda b, h, t: (b, h, 0, 0))
    return pl.pallas_call(
        functools.partial(_hgrn_kernel, blk=HG_BLOCK, nblk=tc // HG_BLOCK),
        out_shape=(jax.ShapeDtypeStruct((batch * seq, BR_WIDTH), BF16),
                   jax.ShapeDtypeStruct((batch, HG_HEADS, HG_DK, HG_DV), F32)),
        grid=(batch, HG_HEADS, nt),
        in_specs=[spec(0), spec(1), spec(2), spec(3), vec, vec, st_spec],
        out_specs=(pl.BlockSpec((tc, HG_DV), lambda b, h, t: (b * nt + t, h)), st_spec),
        scratch_shapes=[pltpu.VMEM((HG_DV, HG_DK), F32)],
        compiler_params=_cparams(("parallel", "parallel", "arbitrary")),
        name="hgrn2",
    )(p_hg, p_hg, p_hg, p_hg, lb.reshape(1, BR_WIDTH), norm_w.reshape(1, BR_WIDTH), s0)


def _rwkv_kernel(r_ref, k_ref, v_ref, lo_ref, pr_ref, pk_ref, pv_ref, plo_ref, mur_ref, muk_ref, muv_ref, mulo_ref,
                 w0_ref, a0_ref, kk_ref, ka_ref, rk_ref, lnw_ref, lnb_ref, w2_ref, a2_ref, g2_ref, s0_ref,
                 y_ref, so_ref, s_scr, prev_r, prev_k, prev_v, prev_lo, *, chunk, nsub, heads):
    tc = pl.program_id(2)
    hd = RW_HD

    @pl.when(tc == 0)
    def _():
        s_scr[...] = s0_ref[0]
        prev_r[...] = pr_ref[0]
        prev_k[...] = pk_ref[0]
        prev_v[...] = pv_ref[0]
        prev_lo[...] = plo_ref[0]

    rows = chunk * nsub

    def token_shift(x_ref, prev_scr, mu_ref):
        x = x_ref[...]
        rolled = pltpu.roll(x, 1, 0)
        first = lax.broadcasted_iota(jnp.int32, x.shape, 0) == 0
        shifted = jnp.where(first, prev_scr[...], rolled)
        prev_scr[...] = x[rows - 1:rows, :]
        return x + (shifted - x) * mu_ref[...]

    r = token_shift(r_ref, prev_r, mur_ref)
    k = token_shift(k_ref, prev_k, muk_ref)
    v = token_shift(v_ref, prev_v, muv_ref)
    lo = token_shift(lo_ref, prev_lo, mulo_ref)
    wd = lo[:, :RW_W_RANK]
    ad = lo[:, RW_W_RANK:RW_W_RANK + RW_A_RANK]
    gd = lo[:, RW_W_RANK + RW_A_RANK:]

    wl = w0_ref[...] + _dot3(jnp.tanh(wd), w2_ref[...], _NN)
    neg = -wl
    softplus = jnp.maximum(neg, 0.0) + jnp.log(1.0 + jnp.exp(-jnp.abs(neg)))
    log_decay = -jnp.exp(-softplus - 0.5)
    lr = _sigmoid(a0_ref[...] + _dot3(ad, a2_ref[...], _NN))
    gate = _dot3(_sigmoid(gd), g2_ref[...], _NN)
    kk_raw = k * kk_ref[...]
    k_mod = k * (1.0 + (lr - 1.0) * ka_ref[...])
    rk_w = r * k_mod * rk_ref[...]

    row = lax.broadcasted_iota(jnp.int32, (chunk, chunk), 0)
    col = lax.broadcasted_iota(jnp.int32, (chunk, chunk), 1)
    lower = row >= col
    strict = row > col
    tri = jnp.where(lower, 1.0, 0.0).astype(BF16)
    eye = jnp.where(row == col, 1.0, 0.0)
    row2 = lax.broadcasted_iota(jnp.int32, (chunk, 2 * chunk), 0)
    col2 = lax.broadcasted_iota(jnp.int32, (chunk, 2 * chunk), 1) & (chunk - 1)
    lower2 = row2 >= col2
    strict2 = row2 > col2

    chains = [(s, j) for s in range(nsub) for j in range(heads)]
    e_last = {}
    ops = {}
    for s in range(nsub):
        rs = slice(s * chunk, (s + 1) * chunk)
        c = _cumsum_rows(tri, log_decay[rs, :])
        c_last = c[chunk - 1:chunk, :]
        e_c = jnp.exp(c)
        e_cprev = jnp.exp(c - log_decay[rs, :])
        e_inv = jnp.exp(-c)
        e_end = jnp.exp(c_last - c)
        e_last[s] = jnp.exp(c_last)
        for j in range(heads):
            cs = slice(j * hd, (j + 1) * hd)
            kkj = kk_raw[rs, cs]
            kkj = kkj / jnp.maximum(jnp.sqrt(jnp.sum(kkj * kkj, axis=-1, keepdims=True)), 1e-12)
            bj = kkj * lr[rs, cs]
            kj = k_mod[rs, cs]
            ops[s, j] = dict(
                a_dec=-kkj * e_cprev[:, cs], r_dec=r[rs, cs] * e_c[:, cs], v=v[rs, cs],
                bk_inv=jnp.concatenate([bj * e_inv[:, cs], kj * e_inv[:, cs]], axis=0),
                b_end=bj * e_end[:, cs], k_end=kj * e_end[:, cs])

    pair = {c: _dot1(jnp.concatenate([ops[c]["a_dec"], ops[c]["r_dec"]], axis=0), ops[c]["bk_inv"], _NT)
            for c in chains}
    m_abk = {c: jnp.where(strict2, pair[c][:chunk, :], 0.0) for c in chains}
    n_rbk = {c: jnp.where(lower2, pair[c][chunk:, :], 0.0) for c in chains}
    mv = {c: _dot1(m_abk[c][:, chunk:], ops[c]["v"], _NN) for c in chains}
    pw = {c: m_abk[c][:, :chunk] for c in chains}
    inv = {c: eye + pw[c] for c in chains}
    for _ in range(chunk.bit_length() - 2):
        pw = {c: _dot1(pw[c], pw[c], _NN) for c in chains}
        inv = {c: inv[c] + _dot1(inv[c], pw[c], _NN) for c in chains}
    t_amv = {c: _dot1(inv[c], jnp.concatenate([ops[c]["a_dec"], mv[c]], axis=1), _NN) for c in chains}
    nr = {c: _dot1(n_rbk[c][:, :chunk], t_amv[c], _NN) for c in chains}
    y_const = {c: nr[c][:, hd:] + _dot1(n_rbk[c][:, chunk:], ops[c]["v"], _NN) for c in chains}
    r_eff = {c: ops[c]["r_dec"] + nr[c][:, :hd] for c in chains}
    gram = {c: _dot1(t_amv[c], ops[c]["b_end"], _TN) for c in chains}
    s_const = {c: gram[c][hd:, :] + _dot1(ops[c]["v"], ops[c]["k_end"], _TN) for c in chains}

    y_heads = {}
    for j in range(heads):
        cs = slice(j * hd, (j + 1) * hd)
        st = s_scr[j]
        for s in range(nsub):
            y_heads[s, j] = _dot1(r_eff[s, j], st, _NT) + y_const[s, j]
            st = st * e_last[s][:, cs] + (_dot1(st, gram[s, j][:hd, :], _NN) + s_const[s, j])
        s_scr[j] = st

    for s in range(nsub):
        rs = slice(s * chunk, (s + 1) * chunk)
        outs = []
        for j in range(heads):
            cs = slice(j * hd, (j + 1) * hd)
            y = y_heads[s, j]
            mean = jnp.mean(y, axis=-1, keepdims=True)
            var = jnp.mean(jnp.square(y - mean), axis=-1, keepdims=True)
            yn = (y - mean) * lax.rsqrt(var + RW_LN_EPS) * lnw_ref[:, cs] + lnb_ref[:, cs]
            yn = yn + jnp.sum(rk_w[rs, cs], axis=-1, keepdims=True) * ops[s, j]["v"]
            outs.append(yn * gate[rs, cs])
        y_ref[rs, :] = jnp.concatenate(outs, axis=1).astype(y_ref.dtype)

    @pl.when(tc == pl.num_programs(2) - 1)
    def _():
        so_ref[0] = s_scr[...]


def _rwkv(p_rw, prev, s0, prm, *, batch, seq, row_off, nsub):
    hg = RW_HEAD_GROUP
    wcols = hg * RW_HD
    rows = RW_CHUNK * nsub
    nt = seq // rows
    rb = row_off // rows
    nsec = BR_WIDTH // wcols
    lo_blk = 3 * nsec
    assert RW_IN - 3 * BR_WIDTH == wcols
    tok = lambda s: pl.BlockSpec((rows, wcols), lambda b, g, t, s=s: (rb + b * nt + t, s * nsec + g))
    tok_lo = pl.BlockSpec((rows, wcols), lambda b, g, t: (rb + b * nt + t, lo_blk))
    prv = lambda s: pl.BlockSpec((1, 1, wcols), lambda b, g, t, s=s: (b, 0, s * nsec + g))
    prv_lo = pl.BlockSpec((1, 1, wcols), lambda b, g, t: (b, 0, lo_blk))
    mu = lambda s: pl.BlockSpec((1, wcols), lambda b, g, t, s=s: (0, s * nsec + g))
    mu_lo = pl.BlockSpec((1, wcols), lambda b, g, t: (0, lo_blk))
    vec = pl.BlockSpec((1, wcols), lambda b, g, t: (0, g))
    lora = lambda rank: pl.BlockSpec((rank, wcols), lambda b, g, t: (0, g))
    st_spec = pl.BlockSpec((1, hg, RW_HD, RW_HD), lambda b, g, t: (b, g, 0, 0))
    prev3 = prev.reshape(batch, 1, RW_IN)
    mu2 = prm["mu"].reshape(1, RW_IN)
    v1 = lambda a: a.reshape(1, BR_WIDTH)
    return pl.pallas_call(
        functools.partial(_rwkv_kernel, chunk=RW_CHUNK, nsub=nsub, heads=hg),
        out_shape=(jax.ShapeDtypeStruct((batch * seq, BR_WIDTH), BF16),
                   jax.ShapeDtypeStruct((batch, RW_HEADS, RW_HD, RW_HD), F32)),
        grid=(batch, RW_HEADS // hg, nt),
        in_specs=[tok(0), tok(1), tok(2), tok_lo, prv(0), prv(1), prv(2), prv_lo, mu(0), mu(1), mu(2), mu_lo,
                  vec, vec, vec, vec, vec, vec, vec, lora(RW_W_RANK), lora(RW_A_RANK), lora(RW_G_RANK), st_spec],
        out_specs=(pl.BlockSpec((rows, wcols), lambda b, g, t: (b * nt + t, g)), st_spec),
        scratch_shapes=[pltpu.VMEM((hg, RW_HD, RW_HD), F32)] + [pltpu.VMEM((1, wcols), F32)] * 4,
        compiler_params=_cparams(("parallel", "parallel", "arbitrary")),
        name="rwkv7",
    )(p_rw, p_rw, p_rw, p_rw, prev3, prev3, prev3, prev3, mu2, mu2, mu2, mu2,
      v1(prm["w0"]), v1(prm["a0"]), v1(prm["kk"]), v1(prm["ka"]), v1(prm["rk"]), v1(prm["lnw"]), v1(prm["lnb"]),
      prm["w2"], prm["a2"], prm["g2"], s0)


def _rope_partner(x):
    width = x.shape[-1]
    half = SW_HD // 2
    lane = lax.broadcasted_iota(jnp.int32, x.shape, x.ndim - 1)
    return jnp.where(lane % SW_HD < half, pltpu.roll(x, width - half, x.ndim - 1), pltpu.roll(x, half, x.ndim - 1))


def _swa_kernel(sink_ref, q_ref, kv_ref, cos_ref, sin_ref, past_ref, o_ref, kvo_ref, buf, *, past_valid):
    c = pl.program_id(1)
    kvw = SW_KV_HEADS * SW_HD

    @pl.when(c == 0)
    def _():
        buf[0:WINDOW, :] = past_ref[0]

    @pl.when(c > 0)
    def _():
        buf[0:WINDOW, :] = buf[CHUNK:CHUNK + WINDOW, :]

    cos = cos_ref[...]
    sin = sin_ref[...]
    kv = kv_ref[...]
    k = kv[:, :kvw]
    v = kv[:, kvw:]
    k_rot = k * cos + _rope_partner(k) * sin
    buf[WINDOW:WINDOW + CHUNK, :kvw] = k_rot
    buf[WINDOW:WINDOW + CHUNK, kvw:] = v
    kvo_ref[:, :kvw] = k_rot
    kvo_ref[:, kvw:] = v

    q = q_ref[...]
    reps = q.shape[1] // cos.shape[1]
    cos_q = jnp.concatenate([cos] * reps, axis=1)
    sin_q = jnp.concatenate([sin] * reps, axis=1)
    q_rot = q * cos_q + _rope_partner(q) * sin_q

    band = WINDOW + CHUNK
    slot = lax.broadcasted_iota(jnp.int32, (Q_PER_KV * CHUNK, band), 1)
    if past_valid:
        valid = None
    else:
        valid = slot >= (WINDOW // CHUNK - c) * CHUNK
    outs = []
    for g in range(SW_KV_HEADS):
        kg = buf[:, g * SW_HD:(g + 1) * SW_HD].astype(BF16)
        vg = buf[:, kvw + g * SW_HD:kvw + (g + 1) * SW_HD].astype(BF16)
        heads = range(g * Q_PER_KV, (g + 1) * Q_PER_KV)
        qg = jnp.concatenate([q_rot[:, h * SW_HD:(h + 1) * SW_HD] for h in heads], axis=0).astype(BF16)
        s = lax.dot_general(qg, kg, (_NT, ((), ())), preferred_element_type=F32) * (SW_HD ** -0.5)
        if valid is not None:
            s = jnp.where(valid, s, MASK_VALUE)
        sink = jnp.concatenate([jnp.full((CHUNK, 1), sink_ref[h], F32) for h in heads], axis=0)
        m = jnp.maximum(jnp.max(s, axis=-1, keepdims=True), sink)
        e = jnp.exp(s - m)
        den = jnp.sum(e, axis=-1, keepdims=True) + jnp.exp(sink - m)
        og = jnp.dot(e.astype(BF16), vg, preferred_element_type=F32) / den
        outs.extend(og[i * CHUNK:(i + 1) * CHUNK, :] for i in range(Q_PER_KV))
    o_ref[...] = jnp.concatenate(outs, axis=1).astype(o_ref.dtype)


def _swa(p_sw, cos_tab, sin_tab, past_kv, sinks, *, batch, seq, row_off, past_valid):
    nc = seq // CHUNK
    rb = row_off // CHUNK
    qw = SW_HEADS * SW_HD
    kvw2 = 2 * SW_KV_HEADS * SW_HD
    grid_spec = pltpu.PrefetchScalarGridSpec(
        num_scalar_prefetch=1,
        grid=(batch, nc),
        in_specs=[pl.BlockSpec((CHUNK, qw), lambda b, c, s: (rb + b * nc + c, 0)),
                  pl.BlockSpec((CHUNK, kvw2), lambda b, c, s: (rb + b * nc + c, qw // kvw2)),
                  pl.BlockSpec((CHUNK, LANES), lambda b, c, s: (c, 0)),
                  pl.BlockSpec((CHUNK, LANES), lambda b, c, s: (c, 0)),
                  pl.BlockSpec((1, WINDOW, kvw2), lambda b, c, s: (b, 0, 0))],
        out_specs=(pl.BlockSpec((CHUNK, qw), lambda b, c, s: (b * nc + c, 0)),
                   pl.BlockSpec((CHUNK, kvw2), lambda b, c, s: (b * nc + c, 0))),
        scratch_shapes=[pltpu.VMEM((WINDOW + CHUNK, kvw2), F32)],
    )
    return pl.pallas_call(
        functools.partial(_swa_kernel, past_valid=past_valid),
        out_shape=(jax.ShapeDtypeStruct((batch * seq, qw), BF16),
                   jax.ShapeDtypeStruct((batch * seq, kvw2), F32)),
        grid_spec=grid_spec,
        compiler_params=_cparams(("parallel", "arbitrary")),
        name="swa",
    )(sinks, p_sw, p_sw, cos_tab, sin_tab, past_kv)


def _rope_tables(pos):
    half = SW_HD // 2
    inv = ROPE_THETA ** (-jnp.arange(half, dtype=F32) / half)
    ang = pos.astype(F32)[:, None] * inv[None, :]
    cos, sin = jnp.cos(ang), jnp.sin(ang)
    reps = LANES // SW_HD
    return jnp.tile(jnp.concatenate([cos, cos], axis=1), (1, reps)), jnp.tile(jnp.concatenate([-sin, sin], axis=1), (1, reps))


def _norm_router_kernel(x_ref, g_ref, wr_ref, h_ref, route_ref):
    x = x_ref[...]
    y = x * lax.rsqrt(jnp.mean(x * x, axis=-1, keepdims=True) + NORM_EPS)
    h = y * g_ref[...]
    h_ref[...] = h
    logits = _dot3(h, wr_ref[...], _NN)
    lane = lax.broadcasted_iota(jnp.int32, logits.shape, 1).astype(F32)
    neg_inf = -jnp.inf
    logits = jnp.where(lane < N_EXPERTS, logits, neg_inf)
    m1 = jnp.max(logits, axis=-1, keepdims=True)
    i1 = jnp.min(jnp.where(logits == m1, lane, float(LANES)), axis=-1, keepdims=True)
    rest = jnp.where(lane == i1, neg_inf, logits)
    m2 = jnp.max(rest, axis=-1, keepdims=True)
    i2 = jnp.min(jnp.where(rest == m2, lane, float(LANES)), axis=-1, keepdims=True)
    e2 = jnp.exp(m2 - m1)
    g1 = 1.0 / (1.0 + e2)
    g2 = e2 / (1.0 + e2)
    route = jnp.where(lane == 0, i1, jnp.where(lane == 1, i2, jnp.where(lane == 2, g1, jnp.where(lane == 3, g2, 0.0))))
    route_ref[...] = route


def _norm_router(x, g, router, *, tm):
    m, d = x.shape
    wr = jnp.zeros((d, LANES), F32).at[:, :N_EXPERTS].set(router)
    return pl.pallas_call(
        _norm_router_kernel,
        out_shape=(jax.ShapeDtypeStruct((m, d), F32), jax.ShapeDtypeStruct((m, LANES), F32)),
        grid=(m // tm,),
        in_specs=[pl.BlockSpec((tm, d), lambda i: (i, 0)), pl.BlockSpec((1, d), lambda i: (0, 0)),
                  pl.BlockSpec((d, LANES), lambda i: (0, 0))],
        out_specs=(pl.BlockSpec((tm, d), lambda i: (i, 0)), pl.BlockSpec((tm, LANES), lambda i: (i, 0))),
        compiler_params=_cparams(("parallel",)),
        name="norm_router",
    )(x, g.reshape(1, d), wr)


def _gather_rows_kernel(idx_ref, src_hbm, o_ref, buf, sem, *, rows):
    def row_copy(r):
        return pltpu.make_async_copy(src_hbm.at[pl.ds(idx_ref[0, 0, r], 1)], buf.at[pl.ds(r, 1)], sem)

    def start(r, carry):
        row_copy(r).start()
        return carry

    lax.fori_loop(0, rows, start, 0, unroll=8)

    def wait(r, carry):
        row_copy(r).wait()
        return carry

    lax.fori_loop(0, rows, wait, 0, unroll=8)
    o_ref[...] = buf[...].astype(o_ref.dtype)


def _gather_rows(src, idx, out_dtype, *, rows):
    n = idx.shape[0]
    nb = n // rows
    d = src.shape[1]
    return pl.pallas_call(
        functools.partial(_gather_rows_kernel, rows=rows),
        out_shape=jax.ShapeDtypeStruct((n, d), out_dtype),
        grid=(nb,),
        in_specs=[pl.BlockSpec((1, 1, rows), lambda i: (i, 0, 0), memory_space=pltpu.SMEM),
                  pl.BlockSpec(memory_space=pl.ANY)],
        out_specs=pl.BlockSpec((rows, d), lambda i: (i, 0)),
        scratch_shapes=[pltpu.VMEM((rows, d), src.dtype), pltpu.SemaphoreType.DMA(())],
        compiler_params=_cparams(("arbitrary",)),
        name="moe_gather",
    )(idx.reshape(nb, 1, rows), src)


def _moe_glu_kernel(be_ref, nv_ref, x_ref, wg_ref, wu_ref, o_ref):
    used = pl.program_id(1) < nv_ref[0]

    @pl.when(used)
    def _():
        x = x_ref[...]
        a = jnp.dot(x, wg_ref[0], preferred_element_type=F32)
        b = jnp.dot(x, wu_ref[0], preferred_element_type=F32)
        o_ref[...] = ((a * _sigmoid(a)) * b).astype(o_ref.dtype)

    @pl.when(jnp.logical_not(used))
    def _():
        o_ref[...] = jnp.zeros_like(o_ref)


def _moe_glu(rows_x, wg, wu, block_e, n_valid, *, tn):
    n, d = rows_x.shape
    nb = n // MOE_ROWS
    f = wg.shape[2]
    blk = lambda b, nv: jnp.minimum(b, nv[0] - 1)
    grid_spec = pltpu.PrefetchScalarGridSpec(
        num_scalar_prefetch=2,
        grid=(f // tn, nb),
        in_specs=[pl.BlockSpec((MOE_ROWS, d), lambda j, b, be, nv: (blk(b, nv), 0)),
                  pl.BlockSpec((1, d, tn), lambda j, b, be, nv: (be[blk(b, nv)], 0, j)),
                  pl.BlockSpec((1, d, tn), lambda j, b, be, nv: (be[blk(b, nv)], 0, j))],
        out_specs=pl.BlockSpec((MOE_ROWS, tn), lambda j, b, be, nv: (b, j)),
    )
    return pl.pallas_call(
        _moe_glu_kernel,
        out_shape=jax.ShapeDtypeStruct((n, f), BF16),
        grid_spec=grid_spec,
        compiler_params=_cparams(("arbitrary", "arbitrary")),
        name="moe_glu",
    )(block_e, n_valid, rows_x, wg, wu)


def _moe_down_kernel(be_ref, nv_ref, x_ref, w_ref, o_ref):
    k = pl.program_id(2)

    used = pl.program_id(1) < nv_ref[0]

    @pl.when(used)
    def _():
        part = jnp.dot(x_ref[...], w_ref[0], preferred_element_type=F32)

        @pl.when(k == 0)
        def _():
            o_ref[...] = part

        @pl.when(k > 0)
        def _():
            o_ref[...] += part

    @pl.when(jnp.logical_not(used))
    def _():
        o_ref[...] = jnp.zeros_like(o_ref)


def _moe_down(act, wd, block_e, n_valid, *, tn, tk):
    n, f = act.shape
    nb = n // MOE_ROWS
    d = wd.shape[2]
    blk = lambda b, nv: jnp.minimum(b, nv[0] - 1)
    grid_spec = pltpu.PrefetchScalarGridSpec(
        num_scalar_prefetch=2,
        grid=(d // tn, nb, f // tk),
        in_specs=[pl.BlockSpec((MOE_ROWS, tk), lambda j, b, k, be, nv: (blk(b, nv), k)),
                  pl.BlockSpec((1, tk, tn), lambda j, b, k, be, nv: (be[blk(b, nv)], k, j))],
        out_specs=pl.BlockSpec((MOE_ROWS, tn), lambda j, b, k, be, nv: (b, j)),
    )
    return pl.pallas_call(
        _moe_down_kernel,
        out_shape=jax.ShapeDtypeStruct((n, d), F32),
        grid_spec=grid_spec,
        compiler_params=_cparams(("arbitrary", "arbitrary", "arbitrary")),
        name="moe_down",
    )(block_e, n_valid, act, wd)


def _combine_kernel(p0_ref, p1_ref, x_ref, gate_ref, out_hbm, o_ref, buf0, buf1, sem, *, rows):
    def copies(r):
        return (pltpu.make_async_copy(out_hbm.at[pl.ds(p0_ref[0, 0, r], 1)], buf0.at[pl.ds(r, 1)], sem.at[0]),
                pltpu.make_async_copy(out_hbm.at[pl.ds(p1_ref[0, 0, r], 1)], buf1.at[pl.ds(r, 1)], sem.at[1]))

    def start(r, carry):
        c0, c1 = copies(r)
        c0.start()
        c1.start()
        return carry

    lax.fori_loop(0, rows, start, 0, unroll=8)

    def wait(r, carry):
        c0, c1 = copies(r)
        c0.wait()
        c1.wait()
        return carry

    lax.fori_loop(0, rows, wait, 0, unroll=8)
    gate = gate_ref[...]
    g0 = gate[:, 2:3]
    g1 = gate[:, 3:4]
    o_ref[...] = x_ref[...] + (buf0[...] * g0 + buf1[...] * g1)


def _combine(x, route, out_rows, pos0, pos1, *, rows):
    m, d = x.shape
    nb = m // rows
    idx_spec = pl.BlockSpec((1, 1, rows), lambda i: (i, 0, 0), memory_space=pltpu.SMEM)
    return pl.pallas_call(
        functools.partial(_combine_kernel, rows=rows),
        out_shape=jax.ShapeDtypeStruct((m, d), F32),
        grid=(nb,),
        in_specs=[idx_spec, idx_spec, pl.BlockSpec((rows, d), lambda i: (i, 0)),
                  pl.BlockSpec((rows, LANES), lambda i: (i, 0)), pl.BlockSpec(memory_space=pl.ANY)],
        out_specs=pl.BlockSpec((rows, d), lambda i: (i, 0)),
        scratch_shapes=[pltpu.VMEM((rows, d), F32), pltpu.VMEM((rows, d), F32), pltpu.SemaphoreType.DMA((2,))],
        compiler_params=_cparams(("arbitrary",)),
        name="moe_combine",
    )(pos0.reshape(nb, 1, rows), pos1.reshape(nb, 1, rows), x, route, out_rows)


def _moe_layer(x, g, router, wg, wu, wd, *, tm, glu_tn, down_tn, down_tk):
    m, d = x.shape
    h, route = _norm_router(x, g, router, tm=tm)
    eid = route[:, :TOP_K].astype(jnp.int32).reshape(-1)
    n_assign = m * TOP_K
    onehot = (eid[:, None] == jnp.arange(N_EXPERTS)[None, :]).astype(jnp.int32)
    rank = jnp.take_along_axis(jnp.cumsum(onehot, axis=0), eid[:, None], axis=1)[:, 0] - 1
    counts = jnp.sum(onehot, axis=0)
    padded = (counts + MOE_ROWS - 1) // MOE_ROWS * MOE_ROWS
    pend = jnp.cumsum(padded)
    pstart = pend - padded
    dest = pstart[eid] + rank
    nb = -(-n_assign // MOE_ROWS) + N_EXPERTS
    tok = jnp.arange(n_assign, dtype=jnp.int32) // TOP_K
    src_tok = jnp.zeros((nb * MOE_ROWS,), jnp.int32).at[dest].set(tok)
    block_e = jnp.minimum(jnp.searchsorted(pend, jnp.arange(nb) * MOE_ROWS, side="right"), N_EXPERTS - 1)
    block_e = block_e.astype(jnp.int32)
    n_valid = (pend[-1] // MOE_ROWS).astype(jnp.int32).reshape(1)
    rows_x = _gather_rows(h, src_tok, BF16, rows=MOE_ROWS)
    act = _moe_glu(rows_x, wg, wu, block_e, n_valid, tn=glu_tn)
    out_rows = _moe_down(act, wd, block_e, n_valid, tn=down_tn, tk=down_tk)
    dest2 = dest.reshape(m, TOP_K).astype(jnp.int32)
    return _combine(x, route, out_rows, dest2[:, 0], dest2[:, 1], rows=256)


def kernel(x_prompt, x_sample, state_hgrn, state_rwkv, state_rwkv_shift, cache_swa_k, cache_swa_v, norm_mix, w_in, hg_lower_bounds, hg_norm, rw_mu, rw_w0, rw_w2, rw_a0, rw_a2, rw_g2, rw_kk, rw_ka, rw_rk, rw_lnw, rw_lnb, sw_sinks, w_branch, w_gate_br, w_out, norm_ffn, ffn_w_gate, ffn_w_up, ffn_w_down, moe_router, moe_w_gate, moe_w_up, moe_w_down, norm_final):
    bp, tp, d = x_prompt.shape
    bs, ts, _ = x_sample.shape
    mp, ms = bp * tp, bs * ts
    m = mp + ms
    depth = w_in.shape[0]
    tm = 768
    tm_norm = 256
    x = jnp.concatenate([x_prompt.reshape(mp, d), x_sample.reshape(ms, d)], axis=0)

    lbs = jax.nn.softmax(hg_lower_bounds.astype(F32), axis=0)
    lbs = jnp.cumsum(lbs, axis=0) - lbs[0]
    cos_p, sin_p = _rope_tables(jnp.arange(tp))
    cos_s, sin_s = _rope_tables(PAST_LEN + jnp.arange(ts))
    kvw = SW_KV_HEADS * SW_HD

    outs = {k: [] for k in ("hg_p", "hg_s", "rw_p", "rw_s", "sh_p", "sh_s", "k_p", "v_p", "k_s", "v_s")}
    for l in range(depth):
        h = _rmsnorm(x, norm_mix[l], BF16, tm_norm)
        w_in_l = w_in[l]
        p_hg = _matmul(h, w_in_l[:, :HG_IN].astype(BF16), tm=tm, tn=512, name="proj_hg")
        p_rw = _matmul(h, w_in_l[:, HG_IN:HG_IN + RW_IN].astype(BF16), tm=tm, tn=256, name="proj_rw")
        p_sw = _matmul(h, w_in_l[:, HG_IN + RW_IN:].astype(BF16), tm=tm, tn=256, name="proj_sw")

        y_hg_p, s_hg_p = _hgrn(p_hg, lbs[l], hg_norm[l], jnp.zeros((bp, HG_HEADS, HG_DK, HG_DV), F32),
                               batch=bp, seq=tp, row_off=0, tc=256)
        y_hg_s, s_hg_s = _hgrn(p_hg, lbs[l], hg_norm[l], state_hgrn[l], batch=bs, seq=ts, row_off=mp, tc=ts)
        rw_prm = dict(mu=rw_mu[l], w0=rw_w0[l], w2=rw_w2[l], a0=rw_a0[l], a2=rw_a2[l], g2=rw_g2[l], kk=rw_kk[l],
                      ka=rw_ka[l], rk=rw_rk[l], lnw=rw_lnw[l], lnb=rw_lnb[l])
        y_rw_p, s_rw_p = _rwkv(p_rw, jnp.zeros((bp, RW_IN), F32), jnp.zeros((bp, RW_HEADS, RW_HD, RW_HD), F32),
                               rw_prm, batch=bp, seq=tp, row_off=0, nsub=4)
        y_rw_s, s_rw_s = _rwkv(p_rw, state_rwkv_shift[l], state_rwkv[l], rw_prm, batch=bs, seq=ts, row_off=mp, nsub=1)
        past_s = jnp.concatenate([cache_swa_k[l].reshape(bs, WINDOW, kvw), cache_swa_v[l].reshape(bs, WINDOW, kvw)],
                                 axis=-1)
        y_sw_p, kv_p = _swa(p_sw, cos_p, sin_p, jnp.zeros((bp, WINDOW, 2 * kvw), F32), sw_sinks[l],
                            batch=bp, seq=tp, row_off=0, past_valid=False)
        y_sw_s, kv_s = _swa(p_sw, cos_s, sin_s, past_s, sw_sinks[l], batch=bs, seq=ts, row_off=mp, past_valid=True)

        branches = [jnp.concatenate([a, b], axis=0) for a, b in ((y_hg_p, y_hg_s), (y_rw_p, y_rw_s), (y_sw_p, y_sw_s))]
        merged = _gate_merge(h, w_gate_br[l].astype(BF16), branches, w_branch[l].astype(BF16), tm=tm, tn=256)
        x = _matmul(merged, w_out[l].astype(BF16), tm=tm, tn=512, res=x, name="w_out")

        j = l // 2
        if l % 2 == 0:
            h2 = _rmsnorm(x, norm_ffn[l], BF16, tm_norm)
            act = _glu(h2, ffn_w_gate[j].astype(BF16), ffn_w_up[j].astype(BF16), tm=tm, tn=512)
            x = _matmul(act, ffn_w_down[j].astype(BF16), tm=tm, tn=1024, tk=2048, res=x, name="ffn_down")
        else:
            x = _moe_layer(x, norm_ffn[l], moe_router[j], moe_w_gate[j].astype(BF16), moe_w_up[j].astype(BF16),
                           moe_w_down[j].astype(BF16), tm=tm_norm, glu_tn=1024, down_tn=2048, down_tk=2048)

        outs["hg_p"].append(s_hg_p)
        outs["hg_s"].append(s_hg_s)
        outs["rw_p"].append(s_rw_p)
        outs["rw_s"].append(s_rw_s)
        outs["sh_p"].append(p_rw[tp - 1:mp:tp])
        outs["sh_s"].append(p_rw[mp + ts - 1::ts])
        kv_p = kv_p.reshape(bp, tp, 2, SW_KV_HEADS, SW_HD)[:, -WINDOW:]
        kv_s = kv_s.reshape(bs, ts, 2, SW_KV_HEADS, SW_HD)
        outs["k_p"].append(kv_p[:, :, 0])
        outs["v_p"].append(kv_p[:, :, 1])
        outs["k_s"].append(kv_s[:, :, 0])
        outs["v_s"].append(kv_s[:, :, 1])

    y_p = _rmsnorm(x, norm_final, F32, tm_norm, 0, mp)
    y_s = _rmsnorm(x, norm_final, F32, tm_norm, mp, ms)
    st = {k: jnp.stack(v) for k, v in outs.items()}
    return (y_p.reshape(bp, tp, d), y_s.reshape(bs, ts, d), st["hg_p"], st["hg_s"], st["rw_p"], st["rw_s"],
            st["sh_p"], st["sh_s"], st["k_p"], st["v_p"], st["k_s"], st["v_s"])
```

```python
import functools

import jax
import jax.numpy as jnp
from jax import lax
from jax.experimental import pallas as pl
from jax.experimental.pallas import tpu as pltpu

F32 = jnp.float32
BF16 = jnp.bfloat16

D_MODEL = 4096
CHUNK = 64
BR_WIDTH = 1024
NORM_EPS = 1e-5
MASK_VALUE = -1e30
HG_HEADS = 8
HG_DK = 128
HG_DV = 128
HG_IN = 4 * BR_WIDTH
RW_HEADS = 16
RW_HD = 64
RW_W_RANK = 64
RW_A_RANK = 64
RW_G_RANK = 128
RW_IN = 3 * BR_WIDTH + RW_W_RANK + RW_A_RANK + RW_G_RANK
RW_LN_EPS = 64e-5
SW_HEADS = 16
SW_KV_HEADS = 2
Q_PER_KV = SW_HEADS // SW_KV_HEADS
SW_HD = 64
WINDOW = 128
ROPE_THETA = 10000.0
SW_IN = SW_HEADS * SW_HD + 2 * SW_KV_HEADS * SW_HD
D_FF = 14336
N_EXPERTS = 8
TOP_K = 2
PAST_LEN = 4096

V7X_VMEM_LIMIT_BYTES = 56 * 1024 * 1024
LANES = 128

HG_BLOCK = 16
RW_CHUNK = 64
RW_HEAD_GROUP = 4
MOE_ROWS = 512
SW_PART_HEADS = 4


def _cparams(semantics):
    return pltpu.CompilerParams(dimension_semantics=semantics, vmem_limit_bytes=V7X_VMEM_LIMIT_BYTES)


def _sigmoid(x):
    return 1.0 / (1.0 + jnp.exp(-x))


def _split2(x):
    hi = x.astype(BF16)
    lo = (x - hi.astype(F32)).astype(BF16)
    return hi, lo


def _dot1(a, b, dims):
    return lax.dot_general(a.astype(BF16), b.astype(BF16), (dims, ((), ())), preferred_element_type=F32)


def _dot3(a, b, dims):
    ah, al = _split2(a)
    bh, bl = _split2(b)
    dg = lambda x, y: lax.dot_general(x, y, (dims, ((), ())), preferred_element_type=F32)
    return dg(ah, bh) + (dg(ah, bl) + dg(al, bh))


_NN = ((1,), (0,))
_NT = ((1,), (1,))
_TN = ((0,), (0,))


def _split3(x):
    hi = x.astype(BF16)
    r1 = x - hi.astype(F32)
    mid = r1.astype(BF16)
    lo = (r1 - mid.astype(F32)).astype(BF16)
    return hi, mid, lo


def _sum_dots(mask, pieces):
    hi, mid, lo = pieces
    d = lambda y: jnp.dot(mask, y, preferred_element_type=F32)
    return d(hi) + (d(mid) + d(lo))


def _cumsum_rows(tri, x):
    return _sum_dots(tri, _split3(x))


def _rmsnorm_kernel(x_ref, g_ref, o_ref):
    x = x_ref[...]
    y = x * lax.rsqrt(jnp.mean(x * x, axis=-1, keepdims=True) + NORM_EPS)
    o_ref[...] = (y * g_ref[...]).astype(o_ref.dtype)


def _rmsnorm(x, g, out_dtype, tm, row_off=0, nrows=None):
    d = x.shape[1]
    m = x.shape[0] if nrows is None else nrows
    rb = row_off // tm
    return pl.pallas_call(
        _rmsnorm_kernel,
        out_shape=jax.ShapeDtypeStruct((m, d), out_dtype),
        grid=(m // tm,),
        in_specs=[pl.BlockSpec((tm, d), lambda i: (rb + i, 0)), pl.BlockSpec((1, d), lambda i: (0, 0))],
        out_specs=pl.BlockSpec((tm, d), lambda i: (i, 0)),
        compiler_params=_cparams(("parallel",)),
        name="rmsnorm",
    )(x, g.reshape(1, d))


def _mm_kernel(x_ref, w_ref, *rest, nk, has_res):
    o_ref = rest[-1]
    part = jnp.dot(x_ref[...], w_ref[...], preferred_element_type=F32)
    if nk == 1:
        if has_res:
            part = part + rest[0][...]
        o_ref[...] = part.astype(o_ref.dtype)
        return
    k = pl.program_id(2)

    @pl.when(k == 0)
    def _():
        o_ref[...] = part + rest[0][...] if has_res else part

    @pl.when(k > 0)
    def _():
        o_ref[...] += part


def _matmul(x, w, *, tm, tn, tk=None, res=None, out_dtype=F32, name="matmul"):
    m, kd = x.shape
    n = w.shape[1]
    tk = kd if tk is None else tk
    nk = kd // tk
    assert m % tm == 0 and n % tn == 0 and kd % tk == 0
    assert nk == 1 or out_dtype == F32
    in_specs = [pl.BlockSpec((tm, tk), lambda i, j, k: (i, k)), pl.BlockSpec((tk, tn), lambda i, j, k: (k, j))]
    args = [x, w]
    if res is not None:
        in_specs.append(pl.BlockSpec((tm, tn), lambda i, j, k: (i, j)))
        args.append(res)
    return pl.pallas_call(
        functools.partial(_mm_kernel, nk=nk, has_res=res is not None),
        out_shape=jax.ShapeDtypeStruct((m, n), out_dtype),
        grid=(m // tm, n // tn, nk),
        in_specs=in_specs,
        out_specs=pl.BlockSpec((tm, tn), lambda i, j, k: (i, j)),
        compiler_params=_cparams(("parallel", "parallel", "arbitrary")),
        name=name,
    )(*args)


def _glu_kernel(x_ref, wg_ref, wu_ref, o_ref):
    x = x_ref[...]
    a = jnp.dot(x, wg_ref[...], preferred_element_type=F32)
    b = jnp.dot(x, wu_ref[...], preferred_element_type=F32)
    o_ref[...] = ((a * _sigmoid(a)) * b).astype(o_ref.dtype)


def _glu(x, wg, wu, *, tm, tn):
    m, kd = x.shape
    n = wg.shape[1]
    return pl.pallas_call(
        _glu_kernel,
        out_shape=jax.ShapeDtypeStruct((m, n), BF16),
        grid=(m // tm, n // tn),
        in_specs=[pl.BlockSpec((tm, kd), lambda i, j: (i, 0)),
                  pl.BlockSpec((kd, tn), lambda i, j: (0, j)),
                  pl.BlockSpec((kd, tn), lambda i, j: (0, j))],
        out_specs=pl.BlockSpec((tm, tn), lambda i, j: (i, j)),
        compiler_params=_cparams(("parallel", "parallel")),
        name="ffn_glu",
    )(x, wg, wu)


def _gate_merge_kernel(h_ref, g0_ref, g1_ref, g2_ref, b0_ref, b1_ref, b2_ref, wb_ref, o_ref):
    h = h_ref[...]
    acc = None
    for n, (g_ref, b_ref) in enumerate(((g0_ref, b0_ref), (g1_ref, b1_ref), (g2_ref, b2_ref))):
        gate = _sigmoid(jnp.dot(h, g_ref[...], preferred_element_type=F32))
        term = gate * jnp.dot(b_ref[...], wb_ref[n], preferred_element_type=F32)
        acc = term if acc is None else acc + term
    o_ref[...] = acc.astype(o_ref.dtype)


def _gate_merge(h, w_gate, branches, w_branch, *, tm, tn):
    m, d = h.shape
    nj = d // tn
    w = branches[0].shape[1]
    gate_spec = lambda n: pl.BlockSpec((d, tn), lambda i, j, n=n: (0, n * nj + j))
    br_spec = pl.BlockSpec((tm, w), lambda i, j: (i, 0))
    return pl.pallas_call(
        _gate_merge_kernel,
        out_shape=jax.ShapeDtypeStruct((m, d), BF16),
        grid=(m // tm, nj),
        in_specs=[pl.BlockSpec((tm, d), lambda i, j: (i, 0)), gate_spec(0), gate_spec(1), gate_spec(2),
                  br_spec, br_spec, br_spec, pl.BlockSpec((3, w, tn), lambda i, j: (0, 0, j))],
        out_specs=pl.BlockSpec((tm, tn), lambda i, j: (i, j)),
        compiler_params=_cparams(("parallel", "parallel")),
        name="gate_merge",
    )(h, w_gate, w_gate, w_gate, branches[0], branches[1], branches[2], w_branch)


def _hgrn_kernel(q_ref, f_ref, i_ref, g_ref, lb_ref, nw_ref, s0_ref, y_ref, so_ref, st_scr, *, blk, nblk):
    tc = pl.program_id(2)

    @pl.when(tc == 0)
    def _():
        st_scr[...] = s0_ref[0, 0].T

    rows = blk * nblk
    row = lax.broadcasted_iota(jnp.int32, (rows, rows), 0)
    col = lax.broadcasted_iota(jnp.int32, (rows, rows), 1)
    in_block = (row - col).astype(jnp.uint32) <= (row & (blk - 1)).astype(jnp.uint32)
    same_block = (row & -blk) == (col & -blk)
    tri = jnp.where(in_block, 1.0, 0.0).astype(BF16)
    ones_bd = jnp.where(same_block, 1.0, 0.0).astype(BF16)
    lb = lb_ref[...]
    q = q_ref[...]
    f = lb + (1.0 - lb) * _sigmoid(f_ref[...])
    log_f = jnp.log(f)
    k = 1.0 - f
    iv = i_ref[...]
    g = g_ref[...]
    pieces = _split3(log_f)
    b = _sum_dots(tri, pieces)
    total = _sum_dots(ones_bd, pieces)
    qe = (q * _sigmoid(q)) * jnp.exp(b)
    ke = k * jnp.exp(-b)
    k_end = k * jnp.exp(total - b)
    decay = jnp.exp(total)
    att = jnp.where(in_block, _dot1(qe, ke, _NT), 0.0)
    o_intra = _dot1(att, iv, _NN)
    blocks = [slice(n * blk, (n + 1) * blk) for n in range(nblk)]
    grams = [_dot1(iv[sl, :], k_end[sl, :], _TN) for sl in blocks]
    states = [st_scr[...]]
    for n, sl in enumerate(blocks):
        states.append(states[n] * decay[sl.start:sl.start + 1, :] + grams[n])
    st_scr[...] = states[nblk]
    o_inter = [_dot1(qe[sl, :], states[n], _NT) for n, sl in enumerate(blocks)]
    o = o_intra + jnp.concatenate(o_inter, axis=0)
    y = o * lax.rsqrt(jnp.mean(o * o, axis=-1, keepdims=True) + NORM_EPS) * nw_ref[...]
    y_ref[...] = (y * (g * _sigmoid(g))).astype(y_ref.dtype)

    @pl.when(tc == pl.num_programs(2) - 1)
    def _():
        so_ref[0, 0] = st_scr[...].T


def _hgrn(p_hg, lb, norm_w, s0, *, batch, seq, row_off, tc):
    nt = seq // tc
    rb = row_off // tc
    sec = BR_WIDTH // HG_DV
    spec = lambda s: pl.BlockSpec((tc, HG_DV), lambda b, h, t, s=s: (rb + b * nt + t, s * sec + h))
    vec = pl.BlockSpec((1, HG_DV), lambda b, h, t: (0, h))
    st_spec = pl.BlockSpec((1, 1, HG_DK, HG_DV), lambda b, h, t: (b, h, 0, 0))
    return pl.pallas_call(
        functools.partial(_hgrn_kernel, blk=HG_BLOCK, nblk=tc // HG_BLOCK),
        out_shape=(jax.ShapeDtypeStruct((batch * seq, BR_WIDTH), BF16),
                   jax.ShapeDtypeStruct((batch, HG_HEADS, HG_DK, HG_DV), F32)),
        grid=(batch, HG_HEADS, nt),
        in_specs=[spec(0), spec(1), spec(2), spec(3), vec, vec, st_spec],
        out_specs=(pl.BlockSpec((tc, HG_DV), lambda b, h, t: (b * nt + t, h)), st_spec),
        scratch_shapes=[pltpu.VMEM((HG_DV, HG_DK), F32)],
        compiler_params=_cparams(("parallel", "parallel", "arbitrary")),
        name="hgrn2",
    )(p_hg, p_hg, p_hg, p_hg, lb.reshape(1, BR_WIDTH), norm_w.reshape(1, BR_WIDTH), s0)


def _rwkv_kernel(r_ref, k_ref, v_ref, lo_ref, pr_ref, pk_ref, pv_ref, plo_ref, mur_ref, muk_ref, muv_ref, mulo_ref,
                 w0_ref, a0_ref, kk_ref, ka_ref, rk_ref, lnw_ref, lnb_ref, w2_ref, a2_ref, g2_ref, s0_ref,
                 y_ref, so_ref, s_scr, prev_r, prev_k, prev_v, prev_lo, *, chunk, nsub, heads):
    tc = pl.program_id(2)
    hd = RW_HD

    @pl.when(tc == 0)
    def _():
        s_scr[...] = s0_ref[0]
        prev_r[...] = pr_ref[0]
        prev_k[...] = pk_ref[0]
        prev_v[...] = pv_ref[0]
        prev_lo[...] = plo_ref[0]

    rows = chunk * nsub

    def token_shift(x_ref, prev_scr, mu_ref):
        x = x_ref[...]
        rolled = pltpu.roll(x, 1, 0)
        first = lax.broadcasted_iota(jnp.int32, x.shape, 0) == 0
        shifted = jnp.where(first, prev_scr[...], rolled)
        prev_scr[...] = x[rows - 1:rows, :]
        return x + (shifted - x) * mu_ref[...]

    r = token_shift(r_ref, prev_r, mur_ref)
    k = token_shift(k_ref, prev_k, muk_ref)
    v = token_shift(v_ref, prev_v, muv_ref)
    lo = token_shift(lo_ref, prev_lo, mulo_ref)
    wd = lo[:, :RW_W_RANK]
    ad = lo[:, RW_W_RANK:RW_W_RANK + RW_A_RANK]
    gd = lo[:, RW_W_RANK + RW_A_RANK:]

    wl = w0_ref[...] + _dot3(jnp.tanh(wd), w2_ref[...], _NN)
    neg = -wl
    softplus = jnp.maximum(neg, 0.0) + jnp.log(1.0 + jnp.exp(-jnp.abs(neg)))
    log_decay = -jnp.exp(-softplus - 0.5)
    lr = _sigmoid(a0_ref[...] + _dot3(ad, a2_ref[...], _NN))
    gate = _dot3(_sigmoid(gd), g2_ref[...], _NN)
    kk_raw = k * kk_ref[...]
    k_mod = k * (1.0 + (lr - 1.0) * ka_ref[...])
    rk_w = r * k_mod * rk_ref[...]

    row = lax.broadcasted_iota(jnp.int32, (chunk, chunk), 0)
    col = lax.broadcasted_iota(jnp.int32, (chunk, chunk), 1)
    lower = row >= col
    strict = row > col
    tri = jnp.where(lower, 1.0, 0.0).astype(BF16)
    eye = jnp.where(row == col, 1.0, 0.0)
    row2 = lax.broadcasted_iota(jnp.int32, (chunk, 2 * chunk), 0)
    col2 = lax.broadcasted_iota(jnp.int32, (chunk, 2 * chunk), 1) & (chunk - 1)
    lower2 = row2 >= col2
    strict2 = row2 > col2

    chains = [(s, j) for s in range(nsub) for j in range(heads)]
    e_last = {}
    ops = {}
    for s in range(nsub):
        rs = slice(s * chunk, (s + 1) * chunk)
        c = _cumsum_rows(tri, log_decay[rs, :])
        c_last = c[chunk - 1:chunk, :]
        e_c = jnp.exp(c)
        e_cprev = jnp.exp(c - log_decay[rs, :])
        e_inv = jnp.exp(-c)
        e_end = jnp.exp(c_last - c)
        e_last[s] = jnp.exp(c_last)
        for j in range(heads):
            cs = slice(j * hd, (j + 1) * hd)
            kkj = kk_raw[rs, cs]
            kkj = kkj / jnp.maximum(jnp.sqrt(jnp.sum(kkj * kkj, axis=-1, keepdims=True)), 1e-12)
            bj = kkj * lr[rs, cs]
            kj = k_mod[rs, cs]
            ops[s, j] = dict(
                a_dec=-kkj * e_cprev[:, cs], r_dec=r[rs, cs] * e_c[:, cs], v=v[rs, cs],
                bk_inv=jnp.concatenate([bj * e_inv[:, cs], kj * e_inv[:, cs]], axis=0),
                b_end=bj * e_end[:, cs], k_end=kj * e_end[:, cs])

    pair = {c: _dot1(jnp.concatenate([ops[c]["a_dec"], ops[c]["r_dec"]], axis=0), ops[c]["bk_inv"], _NT)
            for c in chains}
    m_abk = {c: jnp.where(strict2, pair[c][:chunk, :], 0.0) for c in chains}
    n_rbk = {c: jnp.where(lower2, pair[c][chunk:, :], 0.0) for c in chains}
    mv = {c: _dot1(m_abk[c][:, chunk:], ops[c]["v"], _NN) for c in chains}
    pw = {c: m_abk[c][:, :chunk] for c in chains}
    inv = {c: eye + pw[c] for c in chains}
    for _ in range(chunk.bit_length() - 2):
        pw = {c: _dot1(pw[c], pw[c], _NN) for c in chains}
        inv = {c: inv[c] + _dot1(inv[c], pw[c], _NN) for c in chains}
    t_amv = {c: _dot1(inv[c], jnp.concatenate([ops[c]["a_dec"], mv[c]], axis=1), _NN) for c in chains}
    nr = {c: _dot1(n_rbk[c][:, :chunk], t_amv[c], _NN) for c in chains}
    y_const = {c: nr[c][:, hd:] + _dot1(n_rbk[c][:, chunk:], ops[c]["v"], _NN) for c in chains}
    r_eff = {c: ops[c]["r_dec"] + nr[c][:, :hd] for c in chains}
    gram = {c: _dot1(t_amv[c], ops[c]["b_end"], _TN) for c in chains}
    s_const = {c: gram[c][hd:, :] + _dot1(ops[c]["v"], ops[c]["k_end"], _TN) for c in chains}

    y_heads = {}
    for j in range(heads):
        cs = slice(j * hd, (j + 1) * hd)
        st = s_scr[j]
        for s in range(nsub):
            y_heads[s, j] = _dot1(r_eff[s, j], st, _NT) + y_const[s, j]
            st = st * e_last[s][:, cs] + (_dot1(st, gram[s, j][:hd, :], _NN) + s_const[s, j])
        s_scr[j] = st

    for s in range(nsub):
        rs = slice(s * chunk, (s + 1) * chunk)
        outs = []
        for j in range(heads):
            cs = slice(j * hd, (j + 1) * hd)
            y = y_heads[s, j]
            mean = jnp.mean(y, axis=-1, keepdims=True)
            var = jnp.mean(jnp.square(y - mean), axis=-1, keepdims=True)
            yn = (y - mean) * lax.rsqrt(var + RW_LN_EPS) * lnw_ref[:, cs] + lnb_ref[:, cs]
            yn = yn + jnp.sum(rk_w[rs, cs], axis=-1, keepdims=True) * ops[s, j]["v"]
            outs.append(yn * gate[rs, cs])
        y_ref[rs, :] = jnp.concatenate(outs, axis=1).astype(y_ref.dtype)

    @pl.when(tc == pl.num_programs(2) - 1)
    def _():
        so_ref[0] = s_scr[...]


def _rwkv(p_rw, prev, s0, prm, *, batch, seq, row_off, nsub):
    hg = RW_HEAD_GROUP
    wcols = hg * RW_HD
    rows = RW_CHUNK * nsub
    nt = seq // rows
    rb = row_off // rows
    nsec = BR_WIDTH // wcols
    lo_blk = 3 * nsec
    assert RW_IN - 3 * BR_WIDTH == wcols
    tok = lambda s: pl.BlockSpec((rows, wcols), lambda b, g, t, s=s: (rb + b * nt + t, s * nsec + g))
    tok_lo = pl.BlockSpec((rows, wcols), lambda b, g, t: (rb + b * nt + t, lo_blk))
    prv = lambda s: pl.BlockSpec((1, 1, wcols), lambda b, g, t, s=s: (b, 0, s * nsec + g))
    prv_lo = pl.BlockSpec((1, 1, wcols), lambda b, g, t: (b, 0, lo_blk))
    mu = lambda s: pl.BlockSpec((1, wcols), lambda b, g, t, s=s: (0, s * nsec + g))
    mu_lo = pl.BlockSpec((1, wcols), lambda b, g, t: (0, lo_blk))
    vec = pl.BlockSpec((1, wcols), lambda b, g, t: (0, g))
    lora = lambda rank: pl.BlockSpec((rank, wcols), lambda b, g, t: (0, g))
    st_spec = pl.BlockSpec((1, hg, RW_HD, RW_HD), lambda b, g, t: (b, g, 0, 0))
    prev3 = prev.reshape(batch, 1, RW_IN)
    mu2 = prm["mu"].reshape(1, RW_IN)
    v1 = lambda a: a.reshape(1, BR_WIDTH)
    return pl.pallas_call(
        functools.partial(_rwkv_kernel, chunk=RW_CHUNK, nsub=nsub, heads=hg),
        out_shape=(jax.ShapeDtypeStruct((batch * seq, BR_WIDTH), BF16),
                   jax.ShapeDtypeStruct((batch, RW_HEADS, RW_HD, RW_HD), F32)),
        grid=(batch, RW_HEADS // hg, nt),
        in_specs=[tok(0), tok(1), tok(2), tok_lo, prv(0), prv(1), prv(2), prv_lo, mu(0), mu(1), mu(2), mu_lo,
                  vec, vec, vec, vec, vec, vec, vec, lora(RW_W_RANK), lora(RW_A_RANK), lora(RW_G_RANK), st_spec],
        out_specs=(pl.BlockSpec((rows, wcols), lambda b, g, t: (b * nt + t, g)), st_spec),
        scratch_shapes=[pltpu.VMEM((hg, RW_HD, RW_HD), F32)] + [pltpu.VMEM((1, wcols), F32)] * 4,
        compiler_params=_cparams(("parallel", "parallel", "arbitrary")),
        name="rwkv7",
    )(p_rw, p_rw, p_rw, p_rw, prev3, prev3, prev3, prev3, mu2, mu2, mu2, mu2,
      v1(prm["w0"]), v1(prm["a0"]), v1(prm["kk"]), v1(prm["ka"]), v1(prm["rk"]), v1(prm["lnw"]), v1(prm["lnb"]),
      prm["w2"], prm["a2"], prm["g2"], s0)


def _rope_partner(x):
    width = x.shape[-1]
    half = SW_HD // 2
    lane = lax.broadcasted_iota(jnp.int32, x.shape, x.ndim - 1)
    return jnp.where(lane % SW_HD < half, pltpu.roll(x, width - half, x.ndim - 1), pltpu.roll(x, half, x.ndim - 1))


def _swa_kernel(sink_ref, q_ref, kv_ref, cos_ref, sin_ref, past_ref, o_ref, kvo_ref, buf, *, past_valid):
    c = pl.program_id(1)
    kvw = SW_KV_HEADS * SW_HD

    @pl.when(c == 0)
    def _():
        buf[0:WINDOW, :] = past_ref[0]

    @pl.when(c > 0)
    def _():
        buf[0:WINDOW, :] = buf[CHUNK:CHUNK + WINDOW, :]

    cos = cos_ref[...]
    sin = sin_ref[...]
    kv = kv_ref[...]
    k = kv[:, :kvw]
    v = kv[:, kvw:]
    k_rot = k * cos + _rope_partner(k) * sin
    buf[WINDOW:WINDOW + CHUNK, :kvw] = k_rot
    buf[WINDOW:WINDOW + CHUNK, kvw:] = v
    kvo_ref[:, :kvw] = k_rot
    kvo_ref[:, kvw:] = v

    q = q_ref[...]
    reps = q.shape[1] // cos.shape[1]
    cos_q = jnp.concatenate([cos] * reps, axis=1)
    sin_q = jnp.concatenate([sin] * reps, axis=1)
    q_rot = q * cos_q + _rope_partner(q) * sin_q

    band = WINDOW + CHUNK
    part_rows = SW_PART_HEADS * CHUNK
    slot = lax.broadcasted_iota(jnp.int32, (part_rows, band), 1)
    valid = None if past_valid else slot >= (WINDOW // CHUNK - c) * CHUNK
    kgs = [buf[:, g * SW_HD:(g + 1) * SW_HD].astype(BF16) for g in range(SW_KV_HEADS)]
    vgs = [buf[:, kvw + g * SW_HD:kvw + (g + 1) * SW_HD].astype(BF16) for g in range(SW_KV_HEADS)]
    parts = [range(h0, h0 + SW_PART_HEADS) for h0 in range(0, SW_HEADS, SW_PART_HEADS)]
    qs = [jnp.concatenate([q_rot[:, h * SW_HD:(h + 1) * SW_HD] for h in hs], axis=0).astype(BF16) for hs in parts]
    sinks = [jnp.concatenate([jnp.full((CHUNK, 1), sink_ref[h], F32) for h in hs], axis=0) for hs in parts]
    ss = [lax.dot_general(qp, kgs[hs[0] // Q_PER_KV], (_NT, ((), ())), preferred_element_type=F32) * (SW_HD ** -0.5)
          for qp, hs in zip(qs, parts)]
    if valid is not None:
        ss = [jnp.where(valid, s, MASK_VALUE) for s in ss]
    ms = [jnp.maximum(jnp.max(s, axis=-1, keepdims=True), sink) for s, sink in zip(ss, sinks)]
    es = [jnp.exp(s - m) for s, m in zip(ss, ms)]
    dens = [jnp.sum(e, axis=-1, keepdims=True) + jnp.exp(sink - m) for e, sink, m in zip(es, sinks, ms)]
    ogs = [jnp.dot(e.astype(BF16), vgs[hs[0] // Q_PER_KV], preferred_element_type=F32) / den
           for e, hs, den in zip(es, parts, dens)]
    outs = [og[i * CHUNK:(i + 1) * CHUNK, :] for og in ogs for i in range(SW_PART_HEADS)]
    o_ref[...] = jnp.concatenate(outs, axis=1).astype(o_ref.dtype)


def _swa(p_sw, cos_tab, sin_tab, past_kv, sinks, *, batch, seq, row_off, past_valid):
    nc = seq // CHUNK
    rb = row_off // CHUNK
    qw = SW_HEADS * SW_HD
    kvw2 = 2 * SW_KV_HEADS * SW_HD
    grid_spec = pltpu.PrefetchScalarGridSpec(
        num_scalar_prefetch=1,
        grid=(batch, nc),
        in_specs=[pl.BlockSpec((CHUNK, qw), lambda b, c, s: (rb + b * nc + c, 0)),
                  pl.BlockSpec((CHUNK, kvw2), lambda b, c, s: (rb + b * nc + c, qw // kvw2)),
                  pl.BlockSpec((CHUNK, LANES), lambda b, c, s: (c, 0)),
                  pl.BlockSpec((CHUNK, LANES), lambda b, c, s: (c, 0)),
                  pl.BlockSpec((1, WINDOW, kvw2), lambda b, c, s: (b, 0, 0))],
        out_specs=(pl.BlockSpec((CHUNK, qw), lambda b, c, s: (b * nc + c, 0)),
                   pl.BlockSpec((CHUNK, kvw2), lambda b, c, s: (b * nc + c, 0))),
        scratch_shapes=[pltpu.VMEM((WINDOW + CHUNK, kvw2), F32)],
    )
    return pl.pallas_call(
        functools.partial(_swa_kernel, past_valid=past_valid),
        out_shape=(jax.ShapeDtypeStruct((batch * seq, qw), BF16),
                   jax.ShapeDtypeStruct((batch * seq, kvw2), F32)),
        grid_spec=grid_spec,
        compiler_params=_cparams(("parallel", "arbitrary")),
        name="swa",
    )(sinks, p_sw, p_sw, cos_tab, sin_tab, past_kv)


def _rope_tables(pos):
    half = SW_HD // 2
    inv = ROPE_THETA ** (-jnp.arange(half, dtype=F32) / half)
    ang = pos.astype(F32)[:, None] * inv[None, :]
    cos, sin = jnp.cos(ang), jnp.sin(ang)
    reps = LANES // SW_HD
    return jnp.tile(jnp.concatenate([cos, cos], axis=1), (1, reps)), jnp.tile(jnp.concatenate([-sin, sin], axis=1), (1, reps))


def _norm_router_kernel(x_ref, g_ref, wr_ref, h_ref, route_ref):
    x = x_ref[...]
    y = x * lax.rsqrt(jnp.mean(x * x, axis=-1, keepdims=True) + NORM_EPS)
    h = y * g_ref[...]
    h_ref[...] = h
    logits = _dot3(h, wr_ref[...], _NN)
    lane = lax.broadcasted_iota(jnp.int32, logits.shape, 1).astype(F32)
    neg_inf = -jnp.inf
    logits = jnp.where(lane < N_EXPERTS, logits, neg_inf)
    m1 = jnp.max(logits, axis=-1, keepdims=True)
    i1 = jnp.min(jnp.where(logits == m1, lane, float(LANES)), axis=-1, keepdims=True)
    rest = jnp.where(lane == i1, neg_inf, logits)
    m2 = jnp.max(rest, axis=-1, keepdims=True)
    i2 = jnp.min(jnp.where(rest == m2, lane, float(LANES)), axis=-1, keepdims=True)
    e2 = jnp.exp(m2 - m1)
    g1 = 1.0 / (1.0 + e2)
    g2 = e2 / (1.0 + e2)
    route = jnp.where(lane == 0, i1, jnp.where(lane == 1, i2, jnp.where(lane == 2, g1, jnp.where(lane == 3, g2, 0.0))))
    route_ref[...] = route


def _norm_router(x, g, router, *, tm):
    m, d = x.shape
    wr = jnp.zeros((d, LANES), F32).at[:, :N_EXPERTS].set(router)
    return pl.pallas_call(
        _norm_router_kernel,
        out_shape=(jax.ShapeDtypeStruct((m, d), F32), jax.ShapeDtypeStruct((m, LANES), F32)),
        grid=(m // tm,),
        in_specs=[pl.BlockSpec((tm, d), lambda i: (i, 0)), pl.BlockSpec((1, d), lambda i: (0, 0)),
                  pl.BlockSpec((d, LANES), lambda i: (0, 0))],
        out_specs=(pl.BlockSpec((tm, d), lambda i: (i, 0)), pl.BlockSpec((tm, LANES), lambda i: (i, 0))),
        compiler_params=_cparams(("parallel",)),
        name="norm_router",
    )(x, g.reshape(1, d), wr)


def _gather_rows_kernel(idx_ref, src_hbm, o_ref, buf, sem, *, rows):
    def row_copy(r):
        return pltpu.make_async_copy(src_hbm.at[pl.ds(idx_ref[0, 0, r], 1)], buf.at[pl.ds(r, 1)], sem)

    def start(r, carry):
        row_copy(r).start()
        return carry

    lax.fori_loop(0, rows, start, 0, unroll=8)

    def wait(r, carry):
        row_copy(r).wait()
        return carry

    lax.fori_loop(0, rows, wait, 0, unroll=8)
    o_ref[...] = buf[...].astype(o_ref.dtype)


def _gather_rows(src, idx, out_dtype, *, rows):
    n = idx.shape[0]
    nb = n // rows
    d = src.shape[1]
    return pl.pallas_call(
        functools.partial(_gather_rows_kernel, rows=rows),
        out_shape=jax.ShapeDtypeStruct((n, d), out_dtype),
        grid=(nb,),
        in_specs=[pl.BlockSpec((1, 1, rows), lambda i: (i, 0, 0), memory_space=pltpu.SMEM),
                  pl.BlockSpec(memory_space=pl.ANY)],
        out_specs=pl.BlockSpec((rows, d), lambda i: (i, 0)),
        scratch_shapes=[pltpu.VMEM((rows, d), src.dtype), pltpu.SemaphoreType.DMA(())],
        compiler_params=_cparams(("arbitrary",)),
        name="moe_gather",
    )(idx.reshape(nb, 1, rows), src)


def _moe_glu_kernel(be_ref, nv_ref, x_ref, wg_ref, wu_ref, o_ref):
    used = pl.program_id(1) < nv_ref[0]

    @pl.when(used)
    def _():
        x = x_ref[...]
        a = jnp.dot(x, wg_ref[0].astype(BF16), preferred_element_type=F32)
        b = jnp.dot(x, wu_ref[0].astype(BF16), preferred_element_type=F32)
        o_ref[...] = ((a * _sigmoid(a)) * b).astype(o_ref.dtype)

    @pl.when(jnp.logical_not(used))
    def _():
        o_ref[...] = jnp.zeros_like(o_ref)


def _moe_glu(rows_x, wg, wu, block_e, n_valid, *, tn):
    n, d = rows_x.shape
    nb = n // MOE_ROWS
    f = wg.shape[2]
    blk = lambda b, nv: jnp.minimum(b, nv[0] - 1)
    grid_spec = pltpu.PrefetchScalarGridSpec(
        num_scalar_prefetch=2,
        grid=(f // tn, nb),
        in_specs=[pl.BlockSpec((MOE_ROWS, d), lambda j, b, be, nv: (blk(b, nv), 0)),
                  pl.BlockSpec((1, d, tn), lambda j, b, be, nv: (be[blk(b, nv)], 0, j)),
                  pl.BlockSpec((1, d, tn), lambda j, b, be, nv: (be[blk(b, nv)], 0, j))],
        out_specs=pl.BlockSpec((MOE_ROWS, tn), lambda j, b, be, nv: (b, j)),
    )
    return pl.pallas_call(
        _moe_glu_kernel,
        out_shape=jax.ShapeDtypeStruct((n, f), BF16),
        grid_spec=grid_spec,
        compiler_params=_cparams(("arbitrary", "arbitrary")),
        name="moe_glu",
    )(block_e, n_valid, rows_x, wg, wu)


def _moe_down_kernel(be_ref, nv_ref, x_ref, w_ref, o_ref):
    k = pl.program_id(2)

    used = pl.program_id(1) < nv_ref[0]

    @pl.when(used)
    def _():
        part = jnp.dot(x_ref[...], w_ref[0], preferred_element_type=F32)

        @pl.when(k == 0)
        def _():
            o_ref[...] = part

        @pl.when(k > 0)
        def _():
            o_ref[...] += part

    @pl.when(jnp.logical_not(used))
    def _():
        o_ref[...] = jnp.zeros_like(o_ref)


def _moe_down(act, wd, block_e, n_valid, *, tn, tk):
    n, f = act.shape
    nb = n // MOE_ROWS
    d = wd.shape[2]
    blk = lambda b, nv: jnp.minimum(b, nv[0] - 1)
    grid_spec = pltpu.PrefetchScalarGridSpec(
        num_scalar_prefetch=2,
        grid=(d // tn, nb, f // tk),
        in_specs=[pl.BlockSpec((MOE_ROWS, tk), lambda j, b, k, be, nv: (blk(b, nv), k)),
                  pl.BlockSpec((1, tk, tn), lambda j, b, k, be, nv: (be[blk(b, nv)], k, j))],
        out_specs=pl.BlockSpec((MOE_ROWS, tn), lambda j, b, k, be, nv: (b, j)),
    )
    return pl.pallas_call(
        _moe_down_kernel,
        out_shape=jax.ShapeDtypeStruct((n, d), F32),
        grid_spec=grid_spec,
        compiler_params=_cparams(("arbitrary", "arbitrary", "arbitrary")),
        name="moe_down",
    )(block_e, n_valid, act, wd)


def _combine_kernel(p0_ref, p1_ref, x_ref, gate_ref, out_hbm, o_ref, buf0, buf1, sem, *, rows):
    def copies(r):
        return (pltpu.make_async_copy(out_hbm.at[pl.ds(p0_ref[0, 0, r], 1)], buf0.at[pl.ds(r, 1)], sem.at[0]),
                pltpu.make_async_copy(out_hbm.at[pl.ds(p1_ref[0, 0, r], 1)], buf1.at[pl.ds(r, 1)], sem.at[1]))

    def start(r, carry):
        c0, c1 = copies(r)
        c0.start()
        c1.start()
        return carry

    lax.fori_loop(0, rows, start, 0, unroll=8)

    def wait(r, carry):
        c0, c1 = copies(r)
        c0.wait()
        c1.wait()
        return carry

    lax.fori_loop(0, rows, wait, 0, unroll=8)
    gate = gate_ref[...]
    g0 = gate[:, 2:3]
    g1 = gate[:, 3:4]
    o_ref[...] = x_ref[...] + (buf0[...] * g0 + buf1[...] * g1)


def _combine(x, route, out_rows, pos0, pos1, *, rows):
    m, d = x.shape
    nb = m // rows
    idx_spec = pl.BlockSpec((1, 1, rows), lambda i: (i, 0, 0), memory_space=pltpu.SMEM)
    return pl.pallas_call(
        functools.partial(_combine_kernel, rows=rows),
        out_shape=jax.ShapeDtypeStruct((m, d), F32),
        grid=(nb,),
        in_specs=[idx_spec, idx_spec, pl.BlockSpec((rows, d), lambda i: (i, 0)),
                  pl.BlockSpec((rows, LANES), lambda i: (i, 0)), pl.BlockSpec(memory_space=pl.ANY)],
        out_specs=pl.BlockSpec((rows, d), lambda i: (i, 0)),
        scratch_shapes=[pltpu.VMEM((rows, d), F32), pltpu.VMEM((rows, d), F32), pltpu.SemaphoreType.DMA((2,))],
        compiler_params=_cparams(("arbitrary",)),
        name="moe_combine",
    )(pos0.reshape(nb, 1, rows), pos1.reshape(nb, 1, rows), x, route, out_rows)


def _moe_layer(x, g, router, wg, wu, wd, *, tm, glu_tn, down_tn, down_tk):
    m, d = x.shape
    h, route = _norm_router(x, g, router, tm=tm)
    eid = route[:, :TOP_K].astype(jnp.int32).reshape(-1)
    n_assign = m * TOP_K
    onehot = (eid[:, None] == jnp.arange(N_EXPERTS)[None, :]).astype(jnp.int32)
    rank = jnp.take_along_axis(jnp.cumsum(onehot, axis=0), eid[:, None], axis=1)[:, 0] - 1
    counts = jnp.sum(onehot, axis=0)
    padded = (counts + MOE_ROWS - 1) // MOE_ROWS * MOE_ROWS
    pend = jnp.cumsum(padded)
    pstart = pend - padded
    dest = pstart[eid] + rank
    nb = -(-n_assign // MOE_ROWS) + N_EXPERTS
    tok = jnp.arange(n_assign, dtype=jnp.int32) // TOP_K
    src_tok = jnp.zeros((nb * MOE_ROWS,), jnp.int32).at[dest].set(tok)
    block_e = jnp.minimum(jnp.searchsorted(pend, jnp.arange(nb) * MOE_ROWS, side="right"), N_EXPERTS - 1)
    block_e = block_e.astype(jnp.int32)
    n_valid = (pend[-1] // MOE_ROWS).astype(jnp.int32).reshape(1)
    rows_x = _gather_rows(h, src_tok, BF16, rows=MOE_ROWS)
    act = _moe_glu(rows_x, wg, wu, block_e, n_valid, tn=glu_tn)
    out_rows = _moe_down(act, wd, block_e, n_valid, tn=down_tn, tk=down_tk)
    dest2 = dest.reshape(m, TOP_K).astype(jnp.int32)
    return _combine(x, route, out_rows, dest2[:, 0], dest2[:, 1], rows=256)


def kernel(x_prompt, x_sample, state_hgrn, state_rwkv, state_rwkv_shift, cache_swa_k, cache_swa_v, norm_mix, w_in, hg_lower_bounds, hg_norm, rw_mu, rw_w0, rw_w2, rw_a0, rw_a2, rw_g2, rw_kk, rw_ka, rw_rk, rw_lnw, rw_lnb, sw_sinks, w_branch, w_gate_br, w_out, norm_ffn, ffn_w_gate, ffn_w_up, ffn_w_down, moe_router, moe_w_gate, moe_w_up, moe_w_down, norm_final):
    bp, tp, d = x_prompt.shape
    bs, ts, _ = x_sample.shape
    mp, ms = bp * tp, bs * ts
    m = mp + ms
    depth = w_in.shape[0]
    tm = 768
    tm_norm = 256
    x = jnp.concatenate([x_prompt.reshape(mp, d), x_sample.reshape(ms, d)], axis=0)

    lbs = jax.nn.softmax(hg_lower_bounds.astype(F32), axis=0)
    lbs = jnp.cumsum(lbs, axis=0) - lbs[0]
    cos_p, sin_p = _rope_tables(jnp.arange(tp))
    cos_s, sin_s = _rope_tables(PAST_LEN + jnp.arange(ts))
    kvw = SW_KV_HEADS * SW_HD

    outs = {k: [] for k in ("hg_p", "hg_s", "rw_p", "rw_s", "sh_p", "sh_s", "k_p", "v_p", "k_s", "v_s")}
    for l in range(depth):
        h = _rmsnorm(x, norm_mix[l], BF16, tm_norm)
        w_in_l = w_in[l]
        p_hg = _matmul(h, w_in_l[:, :HG_IN].astype(BF16), tm=tm, tn=512, name="proj_hg")
        p_rw = _matmul(h, w_in_l[:, HG_IN:HG_IN + RW_IN].astype(BF16), tm=tm, tn=256, name="proj_rw")
        p_sw = _matmul(h, w_in_l[:, HG_IN + RW_IN:].astype(BF16), tm=tm, tn=256, name="proj_sw")

        y_hg_p, s_hg_p = _hgrn(p_hg, lbs[l], hg_norm[l], jnp.zeros((bp, HG_HEADS, HG_DK, HG_DV), F32),
                               batch=bp, seq=tp, row_off=0, tc=256)
        y_hg_s, s_hg_s = _hgrn(p_hg, lbs[l], hg_norm[l], state_hgrn[l], batch=bs, seq=ts, row_off=mp, tc=ts)
        rw_prm = dict(mu=rw_mu[l], w0=rw_w0[l], w2=rw_w2[l], a0=rw_a0[l], a2=rw_a2[l], g2=rw_g2[l], kk=rw_kk[l],
                      ka=rw_ka[l], rk=rw_rk[l], lnw=rw_lnw[l], lnb=rw_lnb[l])
        y_rw_p, s_rw_p = _rwkv(p_rw, jnp.zeros((bp, RW_IN), F32), jnp.zeros((bp, RW_HEADS, RW_HD, RW_HD), F32),
                               rw_prm, batch=bp, seq=tp, row_off=0, nsub=4)
        y_rw_s, s_rw_s = _rwkv(p_rw, state_rwkv_shift[l], state_rwkv[l], rw_prm, batch=bs, seq=ts, row_off=mp, nsub=1)
        past_s = jnp.concatenate([cache_swa_k[l].reshape(bs, WINDOW, kvw), cache_swa_v[l].reshape(bs, WINDOW, kvw)],
                                 axis=-1)
        y_sw_p, kv_p = _swa(p_sw, cos_p, sin_p, jnp.zeros((bp, WINDOW, 2 * kvw), F32), sw_sinks[l],
                            batch=bp, seq=tp, row_off=0, past_valid=False)
        y_sw_s, kv_s = _swa(p_sw, cos_s, sin_s, past_s, sw_sinks[l], batch=bs, seq=ts, row_off=mp, past_valid=True)

        branches = [jnp.concatenate([a, b], axis=0) for a, b in ((y_hg_p, y_hg_s), (y_rw_p, y_rw_s), (y_sw_p, y_sw_s))]
        merged = _gate_merge(h, w_gate_br[l].astype(BF16), branches, w_branch[l].astype(BF16), tm=tm, tn=256)
        x = _matmul(merged, w_out[l].astype(BF16), tm=tm, tn=512, res=x, name="w_out")

        j = l // 2
        if l % 2 == 0:
            h2 = _rmsnorm(x, norm_ffn[l], BF16, tm_norm)
            act = _glu(h2, ffn_w_gate[j].astype(BF16), ffn_w_up[j].astype(BF16), tm=tm, tn=512)
            x = _matmul(act, ffn_w_down[j].astype(BF16), tm=tm, tn=1024, tk=3584, res=x, name="ffn_down")
        else:
            x = _moe_layer(x, norm_ffn[l], moe_router[j], moe_w_gate[j], moe_w_up[j],
                           moe_w_down[j].astype(BF16), tm=tm_norm, glu_tn=512, down_tn=2048, down_tk=3584)

        outs["hg_p"].append(s_hg_p)
        outs["hg_s"].append(s_hg_s)
        outs["rw_p"].append(s_rw_p)
        outs["rw_s"].append(s_rw_s)
        outs["sh_p"].append(p_rw[tp - 1:mp:tp])
        outs["sh_s"].append(p_rw[mp + ts - 1::ts])
        kv_p = kv_p.reshape(bp, tp, 2, SW_KV_HEADS, SW_HD)[:, -WINDOW:]
        kv_s = kv_s.reshape(bs, ts, 2, SW_KV_HEADS, SW_HD)
        outs["k_p"].append(kv_p[:, :, 0])
        outs["v_p"].append(kv_p[:, :, 1])
        outs["k_s"].append(kv_s[:, :, 0])
        outs["v_s"].append(kv_s[:, :, 1])

    y_p = _rmsnorm(x, norm_final, F32, tm_norm, 0, mp)
    y_s = _rmsnorm(x, norm_final, F32, tm_norm, mp, ms)
    st = {k: jnp.stack(v) for k, v in outs.items()}
    return (y_p.reshape(bp, tp, d), y_s.reshape(bs, ts, d), st["hg_p"], st["hg_s"], st["rw_p"], st["rw_s"],
            st["sh_p"], st["sh_s"], st["k_p"], st["v_p"], st["k_s"], st["v_s"])
```

```python
import functools

import jax
import jax.numpy as jnp
from jax import lax
from jax.experimental import pallas as pl
from jax.experimental.pallas import tpu as pltpu

F32 = jnp.float32
BF16 = jnp.bfloat16

D_MODEL = 4096
CHUNK = 64
BR_WIDTH = 1024
NORM_EPS = 1e-5
MASK_VALUE = -1e30
HG_HEADS = 8
HG_DK = 128
HG_DV = 128
HG_IN = 4 * BR_WIDTH
RW_HEADS = 16
RW_HD = 64
RW_W_RANK = 64
RW_A_RANK = 64
RW_G_RANK = 128
RW_IN = 3 * BR_WIDTH + RW_W_RANK + RW_A_RANK + RW_G_RANK
RW_LN_EPS = 64e-5
SW_HEADS = 16
SW_KV_HEADS = 2
Q_PER_KV = SW_HEADS // SW_KV_HEADS
SW_HD = 64
WINDOW = 128
ROPE_THETA = 10000.0
SW_IN = SW_HEADS * SW_HD + 2 * SW_KV_HEADS * SW_HD
D_FF = 14336
N_EXPERTS = 8
TOP_K = 2
PAST_LEN = 4096

V7X_VMEM_LIMIT_BYTES = 56 * 1024 * 1024
LANES = 128

HG_BLOCK = 16
HG_HEAD_GROUP = 4
RW_CHUNK = 64
RW_HEAD_GROUP = 4
MOE_ROWS = 512
SW_PART_HEADS = 4


def _cparams(semantics):
    return pltpu.CompilerParams(dimension_semantics=semantics, vmem_limit_bytes=V7X_VMEM_LIMIT_BYTES)


def _sigmoid(x):
    return 1.0 / (1.0 + jnp.exp(-x))


def _split2(x):
    hi = x.astype(BF16)
    lo = (x - hi.astype(F32)).astype(BF16)
    return hi, lo


def _dot1(a, b, dims):
    return lax.dot_general(a.astype(BF16), b.astype(BF16), (dims, ((), ())), preferred_element_type=F32)


def _dot3(a, b, dims):
    ah, al = _split2(a)
    bh, bl = _split2(b)
    dg = lambda x, y: lax.dot_general(x, y, (dims, ((), ())), preferred_element_type=F32)
    return dg(ah, bh) + (dg(ah, bl) + dg(al, bh))


_NN = ((1,), (0,))
_NT = ((1,), (1,))
_TN = ((0,), (0,))


def _split3(x):
    hi = x.astype(BF16)
    r1 = x - hi.astype(F32)
    mid = r1.astype(BF16)
    lo = (r1 - mid.astype(F32)).astype(BF16)
    return hi, mid, lo


def _sum_dots(mask, pieces):
    hi, mid, lo = pieces
    d = lambda y: jnp.dot(mask, y, preferred_element_type=F32)
    return d(hi) + (d(mid) + d(lo))


def _cumsum_rows(tri, x):
    return _sum_dots(tri, _split3(x))


def _rmsnorm_kernel(x_ref, g_ref, o_ref):
    x = x_ref[...]
    y = x * lax.rsqrt(jnp.mean(x * x, axis=-1, keepdims=True) + NORM_EPS)
    o_ref[...] = (y * g_ref[...]).astype(o_ref.dtype)


def _rmsnorm(x, g, out_dtype, tm, row_off=0, nrows=None):
    d = x.shape[1]
    m = x.shape[0] if nrows is None else nrows
    rb = row_off // tm
    return pl.pallas_call(
        _rmsnorm_kernel,
        out_shape=jax.ShapeDtypeStruct((m, d), out_dtype),
        grid=(m // tm,),
        in_specs=[pl.BlockSpec((tm, d), lambda i: (rb + i, 0)), pl.BlockSpec((1, d), lambda i: (0, 0))],
        out_specs=pl.BlockSpec((tm, d), lambda i: (i, 0)),
        compiler_params=_cparams(("parallel",)),
        name="rmsnorm",
    )(x, g.reshape(1, d))


def _mm_kernel(x_ref, w_ref, *rest, nk, has_res):
    o_ref = rest[-1]
    part = jnp.dot(x_ref[...], w_ref[...], preferred_element_type=F32)
    if nk == 1:
        if has_res:
            part = part + rest[0][...]
        o_ref[...] = part.astype(o_ref.dtype)
        return
    k = pl.program_id(2)

    @pl.when(k == 0)
    def _():
        o_ref[...] = part + rest[0][...] if has_res else part

    @pl.when(k > 0)
    def _():
        o_ref[...] += part


def _matmul(x, w, *, tm, tn, tk=None, res=None, out_dtype=F32, name="matmul"):
    m, kd = x.shape
    n = w.shape[1]
    tk = kd if tk is None else tk
    nk = kd // tk
    assert m % tm == 0 and n % tn == 0 and kd % tk == 0
    assert nk == 1 or out_dtype == F32
    in_specs = [pl.BlockSpec((tm, tk), lambda i, j, k: (i, k)), pl.BlockSpec((tk, tn), lambda i, j, k: (k, j))]
    args = [x, w]
    if res is not None:
        in_specs.append(pl.BlockSpec((tm, tn), lambda i, j, k: (i, j)))
        args.append(res)
    return pl.pallas_call(
        functools.partial(_mm_kernel, nk=nk, has_res=res is not None),
        out_shape=jax.ShapeDtypeStruct((m, n), out_dtype),
        grid=(m // tm, n // tn, nk),
        in_specs=in_specs,
        out_specs=pl.BlockSpec((tm, tn), lambda i, j, k: (i, j)),
        compiler_params=_cparams(("parallel", "parallel", "arbitrary")),
        name=name,
    )(*args)


def _glu_kernel(x_ref, wg_ref, wu_ref, o_ref):
    x = x_ref[...]
    a = jnp.dot(x, wg_ref[...], preferred_element_type=F32)
    b = jnp.dot(x, wu_ref[...], preferred_element_type=F32)
    o_ref[...] = ((a * _sigmoid(a)) * b).astype(o_ref.dtype)


def _glu(x, wg, wu, *, tm, tn):
    m, kd = x.shape
    n = wg.shape[1]
    return pl.pallas_call(
        _glu_kernel,
        out_shape=jax.ShapeDtypeStruct((m, n), BF16),
        grid=(m // tm, n // tn),
        in_specs=[pl.BlockSpec((tm, kd), lambda i, j: (i, 0)),
                  pl.BlockSpec((kd, tn), lambda i, j: (0, j)),
                  pl.BlockSpec((kd, tn), lambda i, j: (0, j))],
        out_specs=pl.BlockSpec((tm, tn), lambda i, j: (i, j)),
        compiler_params=_cparams(("parallel", "parallel")),
        name="ffn_glu",
    )(x, wg, wu)


def _gate_merge_kernel(h_ref, g0_ref, g1_ref, g2_ref, b0_ref, b1_ref, b2_ref, wb_ref, o_ref):
    h = h_ref[...]
    acc = None
    for n, (g_ref, b_ref) in enumerate(((g0_ref, b0_ref), (g1_ref, b1_ref), (g2_ref, b2_ref))):
        gate = _sigmoid(jnp.dot(h, g_ref[...], preferred_element_type=F32))
        term = gate * jnp.dot(b_ref[...], wb_ref[n], preferred_element_type=F32)
        acc = term if acc is None else acc + term
    o_ref[...] = acc.astype(o_ref.dtype)


def _gate_merge(h, w_gate, branches, w_branch, *, tm, tn):
    m, d = h.shape
    nj = d // tn
    w = branches[0].shape[1]
    gate_spec = lambda n: pl.BlockSpec((d, tn), lambda i, j, n=n: (0, n * nj + j))
    br_spec = pl.BlockSpec((tm, w), lambda i, j: (i, 0))
    return pl.pallas_call(
        _gate_merge_kernel,
        out_shape=jax.ShapeDtypeStruct((m, d), BF16),
        grid=(m // tm, nj),
        in_specs=[pl.BlockSpec((tm, d), lambda i, j: (i, 0)), gate_spec(0), gate_spec(1), gate_spec(2),
                  br_spec, br_spec, br_spec, pl.BlockSpec((3, w, tn), lambda i, j: (0, 0, j))],
        out_specs=pl.BlockSpec((tm, tn), lambda i, j: (i, j)),
        compiler_params=_cparams(("parallel", "parallel")),
        name="gate_merge",
    )(h, w_gate, w_gate, w_gate, branches[0], branches[1], branches[2], w_branch)


def _hgrn_kernel(q_ref, f_ref, i_ref, g_ref, lb_ref, nw_ref, s0_ref, y_ref, so_ref, st_scr, *, blk, nblk, heads):
    tc = pl.program_id(2)

    @pl.when(tc == 0)
    def _():
        for h in range(heads):
            st_scr[h] = s0_ref[0, h].T

    rows = blk * nblk
    row = lax.broadcasted_iota(jnp.int32, (rows, rows), 0)
    col = lax.broadcasted_iota(jnp.int32, (rows, rows), 1)
    in_block = (row - col).astype(jnp.uint32) <= (row & (blk - 1)).astype(jnp.uint32)
    same_block = (row & -blk) == (col & -blk)
    tri = jnp.where(in_block, 1.0, 0.0).astype(BF16)
    ones_bd = jnp.where(same_block, 1.0, 0.0).astype(BF16)
    lb = lb_ref[...]
    q = q_ref[...]
    f = lb + (1.0 - lb) * _sigmoid(f_ref[...])
    log_f = jnp.log(f)
    k = 1.0 - f
    iv = i_ref[...]
    g = g_ref[...]
    pieces = _split3(log_f)
    b = _sum_dots(tri, pieces)
    total = _sum_dots(ones_bd, pieces)
    qe = (q * _sigmoid(q)) * jnp.exp(b)
    ke = k * jnp.exp(-b)
    k_end = k * jnp.exp(total - b)
    decay = jnp.exp(total)
    lanes = [slice(h * HG_DK, (h + 1) * HG_DK) for h in range(heads)]
    blocks = [slice(n * blk, (n + 1) * blk) for n in range(nblk)]
    att = [jnp.where(in_block, _dot1(qe[:, hs], ke[:, hs], _NT), 0.0) for hs in lanes]
    o_intra = [_dot1(a, iv[:, hs], _NN) for a, hs in zip(att, lanes)]
    grams = [[_dot1(iv[sl, hs], k_end[sl, hs], _TN) for sl in blocks] for hs in lanes]
    o_heads = []
    for h, hs in enumerate(lanes):
        states = [st_scr[h]]
        for n, sl in enumerate(blocks):
            states.append(states[n] * decay[sl.start:sl.start + 1, hs] + grams[h][n])
        st_scr[h] = states[nblk]
        o_inter = [_dot1(qe[sl, hs], states[n], _NT) for n, sl in enumerate(blocks)]
        o = o_intra[h] + jnp.concatenate(o_inter, axis=0)
        o_heads.append(o * lax.rsqrt(jnp.mean(o * o, axis=-1, keepdims=True) + NORM_EPS))
    y = jnp.concatenate(o_heads, axis=1) * nw_ref[...]
    y_ref[...] = (y * (g * _sigmoid(g))).astype(y_ref.dtype)

    @pl.when(tc == pl.num_programs(2) - 1)
    def _():
        for h in range(heads):
            so_ref[0, h] = st_scr[h].T


def _hgrn(p_hg, lb, norm_w, s0, *, batch, seq, row_off, tc):
    nt = seq // tc
    rb = row_off // tc
    hg = HG_HEAD_GROUP
    wcols = hg * HG_DV
    sec = BR_WIDTH // wcols
    spec = lambda s: pl.BlockSpec((tc, wcols), lambda b, h, t, s=s: (rb + b * nt + t, s * sec + h))
    vec = pl.BlockSpec((1, wcols), lambda b, h, t: (0, h))
    st_spec = pl.BlockSpec((1, hg, HG_DK, HG_DV), lambda b, h, t: (b, h, 0, 0))
    return pl.pallas_call(
        functools.partial(_hgrn_kernel, blk=HG_BLOCK, nblk=tc // HG_BLOCK, heads=hg),
        out_shape=(jax.ShapeDtypeStruct((batch * seq, BR_WIDTH), BF16),
                   jax.ShapeDtypeStruct((batch, HG_HEADS, HG_DK, HG_DV), F32)),
        grid=(batch, HG_HEADS // hg, nt),
        in_specs=[spec(0), spec(1), spec(2), spec(3), vec, vec, st_spec],
        out_specs=(pl.BlockSpec((tc, wcols), lambda b, h, t: (b * nt + t, h)), st_spec),
        scratch_shapes=[pltpu.VMEM((hg, HG_DV, HG_DK), F32)],
        compiler_params=_cparams(("parallel", "parallel", "arbitrary")),
        name="hgrn2",
    )(p_hg, p_hg, p_hg, p_hg, lb.reshape(1, BR_WIDTH), norm_w.reshape(1, BR_WIDTH), s0)


def _rwkv_kernel(r_ref, k_ref, v_ref, lo_ref, pr_ref, pk_ref, pv_ref, plo_ref, mur_ref, muk_ref, muv_ref, mulo_ref,
                 w0_ref, a0_ref, kk_ref, ka_ref, rk_ref, lnw_ref, lnb_ref, w2_ref, a2_ref, g2_ref, s0_ref,
                 y_ref, so_ref, s_scr, prev_r, prev_k, prev_v, prev_lo, *, chunk, nsub, heads):
    tc = pl.program_id(2)
    hd = RW_HD

    @pl.when(tc == 0)
    def _():
        s_scr[...] = s0_ref[0]
        prev_r[...] = pr_ref[0]
        prev_k[...] = pk_ref[0]
        prev_v[...] = pv_ref[0]
        prev_lo[...] = plo_ref[0]

    rows = chunk * nsub

    def token_shift(x_ref, prev_scr, mu_ref):
        x = x_ref[...]
        rolled = pltpu.roll(x, 1, 0)
        first = lax.broadcasted_iota(jnp.int32, x.shape, 0) == 0
        shifted = jnp.where(first, prev_scr[...], rolled)
        prev_scr[...] = x[rows - 1:rows, :]
        return x + (shifted - x) * mu_ref[...]

    r = token_shift(r_ref, prev_r, mur_ref)
    k = token_shift(k_ref, prev_k, muk_ref)
    v = token_shift(v_ref, prev_v, muv_ref)
    lo = token_shift(lo_ref, prev_lo, mulo_ref)
    wd = lo[:, :RW_W_RANK]
    ad = lo[:, RW_W_RANK:RW_W_RANK + RW_A_RANK]
    gd = lo[:, RW_W_RANK + RW_A_RANK:]

    wl = w0_ref[...] + _dot3(jnp.tanh(wd), w2_ref[...], _NN)
    neg = -wl
    softplus = jnp.maximum(neg, 0.0) + jnp.log(1.0 + jnp.exp(-jnp.abs(neg)))
    log_decay = -jnp.exp(-softplus - 0.5)
    lr = _sigmoid(a0_ref[...] + _dot3(ad, a2_ref[...], _NN))
    gate = _dot3(_sigmoid(gd), g2_ref[...], _NN)
    kk_raw = k * kk_ref[...]
    k_mod = k * (1.0 + (lr - 1.0) * ka_ref[...])
    rk_w = r * k_mod * rk_ref[...]

    row = lax.broadcasted_iota(jnp.int32, (chunk, chunk), 0)
    col = lax.broadcasted_iota(jnp.int32, (chunk, chunk), 1)
    lower = row >= col
    strict = row > col
    tri = jnp.where(lower, 1.0, 0.0).astype(BF16)
    eye = jnp.where(row == col, 1.0, 0.0)
    row2 = lax.broadcasted_iota(jnp.int32, (chunk, 2 * chunk), 0)
    col2 = lax.broadcasted_iota(jnp.int32, (chunk, 2 * chunk), 1) & (chunk - 1)
    lower2 = row2 >= col2
    strict2 = row2 > col2

    chains = [(s, j) for s in range(nsub) for j in range(heads)]
    e_last = {}
    ops = {}
    for s in range(nsub):
        rs = slice(s * chunk, (s + 1) * chunk)
        c = _cumsum_rows(tri, log_decay[rs, :])
        c_last = c[chunk - 1:chunk, :]
        e_c = jnp.exp(c)
        e_cprev = jnp.exp(c - log_decay[rs, :])
        e_inv = jnp.exp(-c)
        e_end = jnp.exp(c_last - c)
        e_last[s] = jnp.exp(c_last)
        for j in range(heads):
            cs = slice(j * hd, (j + 1) * hd)
            kkj = kk_raw[rs, cs]
            kkj = kkj / jnp.maximum(jnp.sqrt(jnp.sum(kkj * kkj, axis=-1, keepdims=True)), 1e-12)
            bj = kkj * lr[rs, cs]
            kj = k_mod[rs, cs]
            ops[s, j] = dict(
                a_dec=-kkj * e_cprev[:, cs], r_dec=r[rs, cs] * e_c[:, cs], v=v[rs, cs],
                bk_inv=jnp.concatenate([bj * e_inv[:, cs], kj * e_inv[:, cs]], axis=0),
                b_end=bj * e_end[:, cs], k_end=kj * e_end[:, cs])

    pair = {c: _dot1(jnp.concatenate([ops[c]["a_dec"], ops[c]["r_dec"]], axis=0), ops[c]["bk_inv"], _NT)
            for c in chains}
    m_abk = {c: jnp.where(strict2, pair[c][:chunk, :], 0.0) for c in chains}
    n_rbk = {c: jnp.where(lower2, pair[c][chunk:, :], 0.0) for c in chains}
    mv = {c: _dot1(m_abk[c][:, chunk:], ops[c]["v"], _NN) for c in chains}
    pw = {c: m_abk[c][:, :chunk] for c in chains}
    inv = {c: eye + pw[c] for c in chains}
    for _ in range(chunk.bit_length() - 2):
        pw = {c: _dot1(pw[c], pw[c], _NN) for c in chains}
        inv = {c: inv[c] + _dot1(inv[c], pw[c], _NN) for c in chains}
    t_amv = {c: _dot1(inv[c], jnp.concatenate([ops[c]["a_dec"], mv[c]], axis=1), _NN) for c in chains}
    nr = {c: _dot1(n_rbk[c][:, :chunk], t_amv[c], _NN) for c in chains}
    y_const = {c: nr[c][:, hd:] + _dot1(n_rbk[c][:, chunk:], ops[c]["v"], _NN) for c in chains}
    r_eff = {c: ops[c]["r_dec"] + nr[c][:, :hd] for c in chains}
    gram = {c: _dot1(t_amv[c], ops[c]["b_end"], _TN) for c in chains}
    s_const = {c: gram[c][hd:, :] + _dot1(ops[c]["v"], ops[c]["k_end"], _TN) for c in chains}

    y_heads = {}
    for j in range(heads):
        cs = slice(j * hd, (j + 1) * hd)
        st = s_scr[j]
        for s in range(nsub):
            y_heads[s, j] = _dot1(r_eff[s, j], st, _NT) + y_const[s, j]
            st = st * e_last[s][:, cs] + (_dot1(st, gram[s, j][:hd, :], _NN) + s_const[s, j])
        s_scr[j] = st

    for s in range(nsub):
        rs = slice(s * chunk, (s + 1) * chunk)
        outs = []
        for j in range(heads):
            cs = slice(j * hd, (j + 1) * hd)
            y = y_heads[s, j]
            mean = jnp.mean(y, axis=-1, keepdims=True)
            var = jnp.mean(jnp.square(y - mean), axis=-1, keepdims=True)
            yn = (y - mean) * lax.rsqrt(var + RW_LN_EPS) * lnw_ref[:, cs] + lnb_ref[:, cs]
            yn = yn + jnp.sum(rk_w[rs, cs], axis=-1, keepdims=True) * ops[s, j]["v"]
            outs.append(yn * gate[rs, cs])
        y_ref[rs, :] = jnp.concatenate(outs, axis=1).astype(y_ref.dtype)

    @pl.when(tc == pl.num_programs(2) - 1)
    def _():
        so_ref[0] = s_scr[...]


def _rwkv(p_rw, prev, s0, prm, *, batch, seq, row_off, nsub):
    hg = RW_HEAD_GROUP
    wcols = hg * RW_HD
    rows = RW_CHUNK * nsub
    nt = seq // rows
    rb = row_off // rows
    nsec = BR_WIDTH // wcols
    lo_blk = 3 * nsec
    assert RW_IN - 3 * BR_WIDTH == wcols
    tok = lambda s: pl.BlockSpec((rows, wcols), lambda b, g, t, s=s: (rb + b * nt + t, s * nsec + g))
    tok_lo = pl.BlockSpec((rows, wcols), lambda b, g, t: (rb + b * nt + t, lo_blk))
    prv = lambda s: pl.BlockSpec((1, 1, wcols), lambda b, g, t, s=s: (b, 0, s * nsec + g))
    prv_lo = pl.BlockSpec((1, 1, wcols), lambda b, g, t: (b, 0, lo_blk))
    mu = lambda s: pl.BlockSpec((1, wcols), lambda b, g, t, s=s: (0, s * nsec + g))
    mu_lo = pl.BlockSpec((1, wcols), lambda b, g, t: (0, lo_blk))
    vec = pl.BlockSpec((1, wcols), lambda b, g, t: (0, g))
    lora = lambda rank: pl.BlockSpec((rank, wcols), lambda b, g, t: (0, g))
    st_spec = pl.BlockSpec((1, hg, RW_HD, RW_HD), lambda b, g, t: (b, g, 0, 0))
    prev3 = prev.reshape(batch, 1, RW_IN)
    mu2 = prm["mu"].reshape(1, RW_IN)
    v1 = lambda a: a.reshape(1, BR_WIDTH)
    return pl.pallas_call(
        functools.partial(_rwkv_kernel, chunk=RW_CHUNK, nsub=nsub, heads=hg),
        out_shape=(jax.ShapeDtypeStruct((batch * seq, BR_WIDTH), BF16),
                   jax.ShapeDtypeStruct((batch, RW_HEADS, RW_HD, RW_HD), F32)),
        grid=(batch, RW_HEADS // hg, nt),
        in_specs=[tok(0), tok(1), tok(2), tok_lo, prv(0), prv(1), prv(2), prv_lo, mu(0), mu(1), mu(2), mu_lo,
                  vec, vec, vec, vec, vec, vec, vec, lora(RW_W_RANK), lora(RW_A_RANK), lora(RW_G_RANK), st_spec],
        out_specs=(pl.BlockSpec((rows, wcols), lambda b, g, t: (b * nt + t, g)), st_spec),
        scratch_shapes=[pltpu.VMEM((hg, RW_HD, RW_HD), F32)] + [pltpu.VMEM((1, wcols), F32)] * 4,
        compiler_params=_cparams(("parallel", "parallel", "arbitrary")),
        name="rwkv7",
    )(p_rw, p_rw, p_rw, p_rw, prev3, prev3, prev3, prev3, mu2, mu2, mu2, mu2,
      v1(prm["w0"]), v1(prm["a0"]), v1(prm["kk"]), v1(prm["ka"]), v1(prm["rk"]), v1(prm["lnw"]), v1(prm["lnb"]),
      prm["w2"], prm["a2"], prm["g2"], s0)


def _rope_partner(x):
    width = x.shape[-1]
    half = SW_HD // 2
    lane = lax.broadcasted_iota(jnp.int32, x.shape, x.ndim - 1)
    return jnp.where(lane % SW_HD < half, pltpu.roll(x, width - half, x.ndim - 1), pltpu.roll(x, half, x.ndim - 1))


def _swa_kernel(sink_ref, q_ref, kv_ref, cos_ref, sin_ref, past_ref, o_ref, kvo_ref, buf, *, past_valid):
    c = pl.program_id(1)
    kvw = SW_KV_HEADS * SW_HD

    @pl.when(c == 0)
    def _():
        buf[0:WINDOW, :] = past_ref[0]

    @pl.when(c > 0)
    def _():
        buf[0:WINDOW, :] = buf[CHUNK:CHUNK + WINDOW, :]

    cos = cos_ref[...]
    sin = sin_ref[...]
    kv = kv_ref[...]
    k = kv[:, :kvw]
    v = kv[:, kvw:]
    k_rot = k * cos + _rope_partner(k) * sin
    buf[WINDOW:WINDOW + CHUNK, :kvw] = k_rot
    buf[WINDOW:WINDOW + CHUNK, kvw:] = v
    kvo_ref[:, :kvw] = k_rot
    kvo_ref[:, kvw:] = v

    q = q_ref[...]
    reps = q.shape[1] // cos.shape[1]
    cos_q = jnp.concatenate([cos] * reps, axis=1)
    sin_q = jnp.concatenate([sin] * reps, axis=1)
    q_rot = q * cos_q + _rope_partner(q) * sin_q

    band = WINDOW + CHUNK
    part_rows = SW_PART_HEADS * CHUNK
    slot = lax.broadcasted_iota(jnp.int32, (part_rows, band), 1)
    valid = None if past_valid else slot >= (WINDOW // CHUNK - c) * CHUNK
    kgs = [buf[:, g * SW_HD:(g + 1) * SW_HD].astype(BF16) for g in range(SW_KV_HEADS)]
    vgs = [buf[:, kvw + g * SW_HD:kvw + (g + 1) * SW_HD].astype(BF16) for g in range(SW_KV_HEADS)]
    parts = [range(h0, h0 + SW_PART_HEADS) for h0 in range(0, SW_HEADS, SW_PART_HEADS)]
    qs = [jnp.concatenate([q_rot[:, h * SW_HD:(h + 1) * SW_HD] for h in hs], axis=0).astype(BF16) for hs in parts]
    sinks = [jnp.concatenate([jnp.full((CHUNK, 1), sink_ref[h], F32) for h in hs], axis=0) for hs in parts]
    ss = [lax.dot_general(qp, kgs[hs[0] // Q_PER_KV], (_NT, ((), ())), preferred_element_type=F32) * (SW_HD ** -0.5)
          for qp, hs in zip(qs, parts)]
    if valid is not None:
        ss = [jnp.where(valid, s, MASK_VALUE) for s in ss]
    ms = [jnp.maximum(jnp.max(s, axis=-1, keepdims=True), sink) for s, sink in zip(ss, sinks)]
    es = [jnp.exp(s - m) for s, m in zip(ss, ms)]
    dens = [jnp.sum(e, axis=-1, keepdims=True) + jnp.exp(sink - m) for e, sink, m in zip(es, sinks, ms)]
    ogs = [jnp.dot(e.astype(BF16), vgs[hs[0] // Q_PER_KV], preferred_element_type=F32) / den
           for e, hs, den in zip(es, parts, dens)]
    outs = [og[i * CHUNK:(i + 1) * CHUNK, :] for og in ogs for i in range(SW_PART_HEADS)]
    o_ref[...] = jnp.concatenate(outs, axis=1).astype(o_ref.dtype)


def _swa(p_sw, cos_tab, sin_tab, past_kv, sinks, *, batch, seq, row_off, past_valid):
    nc = seq // CHUNK
    rb = row_off // CHUNK
    qw = SW_HEADS * SW_HD
    kvw2 = 2 * SW_KV_HEADS * SW_HD
    grid_spec = pltpu.PrefetchScalarGridSpec(
        num_scalar_prefetch=1,
        grid=(batch, nc),
        in_specs=[pl.BlockSpec((CHUNK, qw), lambda b, c, s: (rb + b * nc + c, 0)),
                  pl.BlockSpec((CHUNK, kvw2), lambda b, c, s: (rb + b * nc + c, qw // kvw2)),
                  pl.BlockSpec((CHUNK, LANES), lambda b, c, s: (c, 0)),
                  pl.BlockSpec((CHUNK, LANES), lambda b, c, s: (c, 0)),
                  pl.BlockSpec((1, WINDOW, kvw2), lambda b, c, s: (b, 0, 0))],
        out_specs=(pl.BlockSpec((CHUNK, qw), lambda b, c, s: (b * nc + c, 0)),
                   pl.BlockSpec((CHUNK, kvw2), lambda b, c, s: (b * nc + c, 0))),
        scratch_shapes=[pltpu.VMEM((WINDOW + CHUNK, kvw2), F32)],
    )
    return pl.pallas_call(
        functools.partial(_swa_kernel, past_valid=past_valid),
        out_shape=(jax.ShapeDtypeStruct((batch * seq, qw), BF16),
                   jax.ShapeDtypeStruct((batch * seq, kvw2), F32)),
        grid_spec=grid_spec,
        compiler_params=_cparams(("parallel", "arbitrary")),
        name="swa",
    )(sinks, p_sw, p_sw, cos_tab, sin_tab, past_kv)


def _rope_tables(pos):
    half = SW_HD // 2
    inv = ROPE_THETA ** (-jnp.arange(half, dtype=F32) / half)
    ang = pos.astype(F32)[:, None] * inv[None, :]
    cos, sin = jnp.cos(ang), jnp.sin(ang)
    reps = LANES // SW_HD
    return jnp.tile(jnp.concatenate([cos, cos], axis=1), (1, reps)), jnp.tile(jnp.concatenate([-sin, sin], axis=1), (1, reps))


def _norm_router_kernel(x_ref, g_ref, wr_ref, h_ref, route_ref):
    x = x_ref[...]
    y = x * lax.rsqrt(jnp.mean(x * x, axis=-1, keepdims=True) + NORM_EPS)
    h = y * g_ref[...]
    h_ref[...] = h
    logits = _dot3(h, wr_ref[...], _NN)
    lane = lax.broadcasted_iota(jnp.int32, logits.shape, 1).astype(F32)
    neg_inf = -jnp.inf
    logits = jnp.where(lane < N_EXPERTS, logits, neg_inf)
    m1 = jnp.max(logits, axis=-1, keepdims=True)
    i1 = jnp.min(jnp.where(logits == m1, lane, float(LANES)), axis=-1, keepdims=True)
    rest = jnp.where(lane == i1, neg_inf, logits)
    m2 = jnp.max(rest, axis=-1, keepdims=True)
    i2 = jnp.min(jnp.where(rest == m2, lane, float(LANES)), axis=-1, keepdims=True)
    e2 = jnp.exp(m2 - m1)
    g1 = 1.0 / (1.0 + e2)
    g2 = e2 / (1.0 + e2)
    route = jnp.where(lane == 0, i1, jnp.where(lane == 1, i2, jnp.where(lane == 2, g1, jnp.where(lane == 3, g2, 0.0))))
    route_ref[...] = route


def _norm_router(x, g, router, *, tm):
    m, d = x.shape
    wr = jnp.zeros((d, LANES), F32).at[:, :N_EXPERTS].set(router)
    return pl.pallas_call(
        _norm_router_kernel,
        out_shape=(jax.ShapeDtypeStruct((m, d), F32), jax.ShapeDtypeStruct((m, LANES), F32)),
        grid=(m // tm,),
        in_specs=[pl.BlockSpec((tm, d), lambda i: (i, 0)), pl.BlockSpec((1, d), lambda i: (0, 0)),
                  pl.BlockSpec((d, LANES), lambda i: (0, 0))],
        out_specs=(pl.BlockSpec((tm, d), lambda i: (i, 0)), pl.BlockSpec((tm, LANES), lambda i: (i, 0))),
        compiler_params=_cparams(("parallel",)),
        name="norm_router",
    )(x, g.reshape(1, d), wr)


def _gather_rows_kernel(idx_ref, nxt_ref, src_hbm, o_ref, buf, sem, *, rows):
    i = pl.program_id(0)
    slot = i % 2

    def row_copy(src_row, s, r):
        return pltpu.make_async_copy(src_hbm.at[pl.ds(src_row, 1)], buf.at[s, pl.ds(r, 1)], sem.at[s])

    def issue(ref, s):
        def start(r, carry):
            row_copy(ref[0, 0, r], s, r).start()
            return carry

        lax.fori_loop(0, rows, start, 0, unroll=8)

    @pl.when(i == 0)
    def _():
        issue(idx_ref, 0)

    @pl.when(i + 1 < pl.num_programs(0))
    def _():
        issue(nxt_ref, 1 - slot)

    def wait(r, carry):
        row_copy(0, slot, r).wait()
        return carry

    lax.fori_loop(0, rows, wait, 0, unroll=8)
    o_ref[...] = buf[slot].astype(o_ref.dtype)


def _gather_rows(src, idx, out_dtype, *, rows):
    n = idx.shape[0]
    nb = n // rows
    d = src.shape[1]
    return pl.pallas_call(
        functools.partial(_gather_rows_kernel, rows=rows),
        out_shape=jax.ShapeDtypeStruct((n, d), out_dtype),
        grid=(nb,),
        in_specs=[pl.BlockSpec((1, 1, rows), lambda i: (i, 0, 0), memory_space=pltpu.SMEM),
                  pl.BlockSpec((1, 1, rows), lambda i: (jnp.minimum(i + 1, nb - 1), 0, 0), memory_space=pltpu.SMEM),
                  pl.BlockSpec(memory_space=pl.ANY)],
        out_specs=pl.BlockSpec((rows, d), lambda i: (i, 0)),
        scratch_shapes=[pltpu.VMEM((2, rows, d), src.dtype), pltpu.SemaphoreType.DMA((2,))],
        compiler_params=_cparams(("arbitrary",)),
        name="moe_gather",
    )(idx.reshape(nb, 1, rows), idx.reshape(nb, 1, rows), src)


def _moe_glu_kernel(be_ref, nv_ref, x_ref, wg_ref, wu_ref, o_ref):
    used = pl.program_id(1) < nv_ref[0]

    @pl.when(used)
    def _():
        x = x_ref[...]
        a = jnp.dot(x, wg_ref[0].astype(BF16), preferred_element_type=F32)
        b = jnp.dot(x, wu_ref[0].astype(BF16), preferred_element_type=F32)
        o_ref[...] = ((a * _sigmoid(a)) * b).astype(o_ref.dtype)

    @pl.when(jnp.logical_not(used))
    def _():
        o_ref[...] = jnp.zeros_like(o_ref)


def _moe_glu(rows_x, wg, wu, block_e, n_valid, *, tn):
    n, d = rows_x.shape
    nb = n // MOE_ROWS
    f = wg.shape[2]
    blk = lambda b, nv: jnp.minimum(b, nv[0] - 1)
    grid_spec = pltpu.PrefetchScalarGridSpec(
        num_scalar_prefetch=2,
        grid=(f // tn, nb),
        in_specs=[pl.BlockSpec((MOE_ROWS, d), lambda j, b, be, nv: (blk(b, nv), 0)),
                  pl.BlockSpec((1, d, tn), lambda j, b, be, nv: (be[blk(b, nv)], 0, j)),
                  pl.BlockSpec((1, d, tn), lambda j, b, be, nv: (be[blk(b, nv)], 0, j))],
        out_specs=pl.BlockSpec((MOE_ROWS, tn), lambda j, b, be, nv: (b, j)),
    )
    return pl.pallas_call(
        _moe_glu_kernel,
        out_shape=jax.ShapeDtypeStruct((n, f), BF16),
        grid_spec=grid_spec,
        compiler_params=_cparams(("arbitrary", "arbitrary")),
        name="moe_glu",
    )(block_e, n_valid, rows_x, wg, wu)


def _moe_down_kernel(be_ref, nv_ref, x_ref, w_ref, o_ref):
    k = pl.program_id(2)

    used = pl.program_id(1) < nv_ref[0]

    @pl.when(used)
    def _():
        part = jnp.dot(x_ref[...], w_ref[0], preferred_element_type=F32)

        @pl.when(k == 0)
        def _():
            o_ref[...] = part

        @pl.when(k > 0)
        def _():
            o_ref[...] += part

    @pl.when(jnp.logical_not(used))
    def _():
        o_ref[...] = jnp.zeros_like(o_ref)


def _moe_down(act, wd, block_e, n_valid, *, tn, tk):
    n, f = act.shape
    nb = n // MOE_ROWS
    d = wd.shape[2]
    blk = lambda b, nv: jnp.minimum(b, nv[0] - 1)
    grid_spec = pltpu.PrefetchScalarGridSpec(
        num_scalar_prefetch=2,
        grid=(d // tn, nb, f // tk),
        in_specs=[pl.BlockSpec((MOE_ROWS, tk), lambda j, b, k, be, nv: (blk(b, nv), k)),
                  pl.BlockSpec((1, tk, tn), lambda j, b, k, be, nv: (be[blk(b, nv)], k, j))],
        out_specs=pl.BlockSpec((MOE_ROWS, tn), lambda j, b, k, be, nv: (b, j)),
    )
    return pl.pallas_call(
        _moe_down_kernel,
        out_shape=jax.ShapeDtypeStruct((n, d), F32),
        grid_spec=grid_spec,
        compiler_params=_cparams(("arbitrary", "arbitrary", "arbitrary")),
        name="moe_down",
    )(block_e, n_valid, act, wd)


def _combine_kernel(p0_ref, p1_ref, n0_ref, n1_ref, x_ref, gate_ref, out_hbm, o_ref, buf, sem, *, rows):
    i = pl.program_id(0)
    slot = i % 2

    def row_copy(src_row, s, k, r):
        return pltpu.make_async_copy(out_hbm.at[pl.ds(src_row, 1)], buf.at[s, k, pl.ds(r, 1)], sem.at[s, k])

    def issue(ref0, ref1, s):
        def start(r, carry):
            row_copy(ref0[0, 0, r], s, 0, r).start()
            row_copy(ref1[0, 0, r], s, 1, r).start()
            return carry

        lax.fori_loop(0, rows, start, 0, unroll=8)

    @pl.when(i == 0)
    def _():
        issue(p0_ref, p1_ref, 0)

    @pl.when(i + 1 < pl.num_programs(0))
    def _():
        issue(n0_ref, n1_ref, 1 - slot)

    def wait(r, carry):
        row_copy(0, slot, 0, r).wait()
        row_copy(0, slot, 1, r).wait()
        return carry

    lax.fori_loop(0, rows, wait, 0, unroll=8)
    gate = gate_ref[...]
    g0 = gate[:, 2:3]
    g1 = gate[:, 3:4]
    o_ref[...] = x_ref[...] + (buf[slot, 0] * g0 + buf[slot, 1] * g1)


def _combine(x, route, out_rows, pos0, pos1, *, rows):
    m, d = x.shape
    nb = m // rows
    idx_spec = pl.BlockSpec((1, 1, rows), lambda i: (i, 0, 0), memory_space=pltpu.SMEM)
    nxt_spec = pl.BlockSpec((1, 1, rows), lambda i: (jnp.minimum(i + 1, nb - 1), 0, 0), memory_space=pltpu.SMEM)
    p0 = pos0.reshape(nb, 1, rows)
    p1 = pos1.reshape(nb, 1, rows)
    return pl.pallas_call(
        functools.partial(_combine_kernel, rows=rows),
        out_shape=jax.ShapeDtypeStruct((m, d), F32),
        grid=(nb,),
        in_specs=[idx_spec, idx_spec, nxt_spec, nxt_spec, pl.BlockSpec((rows, d), lambda i: (i, 0)),
                  pl.BlockSpec((rows, LANES), lambda i: (i, 0)), pl.BlockSpec(memory_space=pl.ANY)],
        out_specs=pl.BlockSpec((rows, d), lambda i: (i, 0)),
        scratch_shapes=[pltpu.VMEM((2, TOP_K, rows, d), F32), pltpu.SemaphoreType.DMA((2, TOP_K))],
        compiler_params=_cparams(("arbitrary",)),
        name="moe_combine",
    )(p0, p1, p0, p1, x, route, out_rows)


def _moe_layer(x, g, router, wg, wu, wd, *, tm, glu_tn, down_tn, down_tk):
    m, d = x.shape
    h, route = _norm_router(x, g, router, tm=tm)
    eid = route[:, :TOP_K].astype(jnp.int32).reshape(-1)
    n_assign = m * TOP_K
    onehot = (eid[:, None] == jnp.arange(N_EXPERTS)[None, :]).astype(jnp.int32)
    rank = jnp.take_along_axis(jnp.cumsum(onehot, axis=0), eid[:, None], axis=1)[:, 0] - 1
    counts = jnp.sum(onehot, axis=0)
    padded = (counts + MOE_ROWS - 1) // MOE_ROWS * MOE_ROWS
    pend = jnp.cumsum(padded)
    pstart = pend - padded
    dest = pstart[eid] + rank
    nb = -(-n_assign // MOE_ROWS) + N_EXPERTS
    tok = jnp.arange(n_assign, dtype=jnp.int32) // TOP_K
    src_tok = jnp.zeros((nb * MOE_ROWS,), jnp.int32).at[dest].set(tok)
    block_e = jnp.minimum(jnp.searchsorted(pend, jnp.arange(nb) * MOE_ROWS, side="right"), N_EXPERTS - 1)
    block_e = block_e.astype(jnp.int32)
    n_valid = (pend[-1] // MOE_ROWS).astype(jnp.int32).reshape(1)
    rows_x = _gather_rows(h, src_tok, BF16, rows=MOE_ROWS)
    act = _moe_glu(rows_x, wg, wu, block_e, n_valid, tn=glu_tn)
    out_rows = _moe_down(act, wd, block_e, n_valid, tn=down_tn, tk=down_tk)
    dest2 = dest.reshape(m, TOP_K).astype(jnp.int32)
    return _combine(x, route, out_rows, dest2[:, 0], dest2[:, 1], rows=256)


def kernel(x_prompt, x_sample, state_hgrn, state_rwkv, state_rwkv_shift, cache_swa_k, cache_swa_v, norm_mix, w_in, hg_lower_bounds, hg_norm, rw_mu, rw_w0, rw_w2, rw_a0, rw_a2, rw_g2, rw_kk, rw_ka, rw_rk, rw_lnw, rw_lnb, sw_sinks, w_branch, w_gate_br, w_out, norm_ffn, ffn_w_gate, ffn_w_up, ffn_w_down, moe_router, moe_w_gate, moe_w_up, moe_w_down, norm_final):
    bp, tp, d = x_prompt.shape
    bs, ts, _ = x_sample.shape
    mp, ms = bp * tp, bs * ts
    m = mp + ms
    depth = w_in.shape[0]
    tm = 768
    tm_norm = 256
    x = jnp.concatenate([x_prompt.reshape(mp, d), x_sample.reshape(ms, d)], axis=0)

    lbs = jax.nn.softmax(hg_lower_bounds.astype(F32), axis=0)
    lbs = jnp.cumsum(lbs, axis=0) - lbs[0]
    cos_p, sin_p = _rope_tables(jnp.arange(tp))
    cos_s, sin_s = _rope_tables(PAST_LEN + jnp.arange(ts))
    kvw = SW_KV_HEADS * SW_HD

    outs = {k: [] for k in ("hg_p", "hg_s", "rw_p", "rw_s", "sh_p", "sh_s", "k_p", "v_p", "k_s", "v_s")}
    for l in range(depth):
        h = _rmsnorm(x, norm_mix[l], BF16, tm_norm)
        w_in_l = w_in[l]
        p_hg = _matmul(h, w_in_l[:, :HG_IN].astype(BF16), tm=tm, tn=512, name="proj_hg")
        p_rw = _matmul(h, w_in_l[:, HG_IN:HG_IN + RW_IN].astype(BF16), tm=tm, tn=256, name="proj_rw")
        p_sw = _matmul(h, w_in_l[:, HG_IN + RW_IN:].astype(BF16), tm=tm, tn=256, name="proj_sw")

        y_hg_p, s_hg_p = _hgrn(p_hg, lbs[l], hg_norm[l], jnp.zeros((bp, HG_HEADS, HG_DK, HG_DV), F32),
                               batch=bp, seq=tp, row_off=0, tc=256)
        y_hg_s, s_hg_s = _hgrn(p_hg, lbs[l], hg_norm[l], state_hgrn[l], batch=bs, seq=ts, row_off=mp, tc=ts)
        rw_prm = dict(mu=rw_mu[l], w0=rw_w0[l], w2=rw_w2[l], a0=rw_a0[l], a2=rw_a2[l], g2=rw_g2[l], kk=rw_kk[l],
                      ka=rw_ka[l], rk=rw_rk[l], lnw=rw_lnw[l], lnb=rw_lnb[l])
        y_rw_p, s_rw_p = _rwkv(p_rw, jnp.zeros((bp, RW_IN), F32), jnp.zeros((bp, RW_HEADS, RW_HD, RW_HD), F32),
                               rw_prm, batch=bp, seq=tp, row_off=0, nsub=8)
        y_rw_s, s_rw_s = _rwkv(p_rw, state_rwkv_shift[l], state_rwkv[l], rw_prm, batch=bs, seq=ts, row_off=mp, nsub=1)
        past_s = jnp.concatenate([cache_swa_k[l].reshape(bs, WINDOW, kvw), cache_swa_v[l].reshape(bs, WINDOW, kvw)],
                                 axis=-1)
        y_sw_p, kv_p = _swa(p_sw, cos_p, sin_p, jnp.zeros((bp, WINDOW, 2 * kvw), F32), sw_sinks[l],
                            batch=bp, seq=tp, row_off=0, past_valid=False)
        y_sw_s, kv_s = _swa(p_sw, cos_s, sin_s, past_s, sw_sinks[l], batch=bs, seq=ts, row_off=mp, past_valid=True)

        branches = [jnp.concatenate([a, b], axis=0) for a, b in ((y_hg_p, y_hg_s), (y_rw_p, y_rw_s), (y_sw_p, y_sw_s))]
        merged = _gate_merge(h, w_gate_br[l].astype(BF16), branches, w_branch[l].astype(BF16), tm=tm, tn=256)
        x = _matmul(merged, w_out[l].astype(BF16), tm=tm, tn=512, res=x, name="w_out")

        j = l // 2
        if l % 2 == 0:
            h2 = _rmsnorm(x, norm_ffn[l], BF16, tm_norm)
            act = _glu(h2, ffn_w_gate[j].astype(BF16), ffn_w_up[j].astype(BF16), tm=tm, tn=512)
            x = _matmul(act, ffn_w_down[j].astype(BF16), tm=tm, tn=1024, tk=3584, res=x, name="ffn_down")
        else:
            x = _moe_layer(x, norm_ffn[l], moe_router[j], moe_w_gate[j], moe_w_up[j],
                           moe_w_down[j].astype(BF16), tm=tm_norm, glu_tn=512, down_tn=2048, down_tk=3584)

        outs["hg_p"].append(s_hg_p)
        outs["hg_s"].append(s_hg_s)
        outs["rw_p"].append(s_rw_p)
        outs["rw_s"].append(s_rw_s)
        outs["sh_p"].append(p_rw[tp - 1:mp:tp])
        outs["sh_s"].append(p_rw[mp + ts - 1::ts])
        kv_p = kv_p.reshape(bp, tp, 2, SW_KV_HEADS, SW_HD)[:, -WINDOW:]
        kv_s = kv_s.reshape(bs, ts, 2, SW_KV_HEADS, SW_HD)
        outs["k_p"].append(kv_p[:, :, 0])
        outs["v_p"].append(kv_p[:, :, 1])
        outs["k_s"].append(kv_s[:, :, 0])
        outs["v_s"].append(kv_s[:, :, 1])

    y_p = _rmsnorm(x, norm_final, F32, tm_norm, 0, mp)
    y_s = _rmsnorm(x, norm_final, F32, tm_norm, mp, ms)
    st = {k: jnp.stack(v) for k, v in outs.items()}
    return (y_p.reshape(bp, tp, d), y_s.reshape(bs, ts, d), st["hg_p"], st["hg_s"], st["rw_p"], st["rw_s"],
            st["sh_p"], st["sh_s"], st["k_p"], st["v_p"], st["k_s"], st["v_s"])
```

```python
import functools

import jax
import jax.numpy as jnp
from jax import lax
from jax.experimental import pallas as pl
from jax.experimental.pallas import tpu as pltpu

F32 = jnp.float32
BF16 = jnp.bfloat16

D_MODEL = 4096
CHUNK = 64
BR_WIDTH = 1024
NORM_EPS = 1e-5
MASK_VALUE = -1e30
HG_HEADS = 8
HG_DK = 128
HG_DV = 128
HG_IN = 4 * BR_WIDTH
RW_HEADS = 16
RW_HD = 64
RW_W_RANK = 64
RW_A_RANK = 64
RW_G_RANK = 128
RW_IN = 3 * BR_WIDTH + RW_W_RANK + RW_A_RANK + RW_G_RANK
RW_LN_EPS = 64e-5
SW_HEADS = 16
SW_KV_HEADS = 2
Q_PER_KV = SW_HEADS // SW_KV_HEADS
SW_HD = 64
WINDOW = 128
ROPE_THETA = 10000.0
SW_IN = SW_HEADS * SW_HD + 2 * SW_KV_HEADS * SW_HD
D_FF = 14336
N_EXPERTS = 8
TOP_K = 2
PAST_LEN = 4096

V7X_VMEM_LIMIT_BYTES = 56 * 1024 * 1024
LANES = 128

HG_BLOCK = 16
HG_HEAD_GROUP = 4
RW_CHUNK = 64
RW_HEAD_GROUP = 4
MOE_ROWS = 512
SW_PART_HEADS = 4


def _cparams(semantics):
    return pltpu.CompilerParams(dimension_semantics=semantics, vmem_limit_bytes=V7X_VMEM_LIMIT_BYTES)


def _sigmoid(x):
    return 1.0 / (1.0 + jnp.exp(-x))


def _split2(x):
    hi = x.astype(BF16)
    lo = (x - hi.astype(F32)).astype(BF16)
    return hi, lo


def _dot1(a, b, dims):
    return lax.dot_general(a.astype(BF16), b.astype(BF16), (dims, ((), ())), preferred_element_type=F32)


def _dot3(a, b, dims):
    ah, al = _split2(a)
    bh, bl = _split2(b)
    dg = lambda x, y: lax.dot_general(x, y, (dims, ((), ())), preferred_element_type=F32)
    return dg(ah, bh) + (dg(ah, bl) + dg(al, bh))


_NN = ((1,), (0,))
_NT = ((1,), (1,))
_TN = ((0,), (0,))


def _split3(x):
    hi = x.astype(BF16)
    r1 = x - hi.astype(F32)
    mid = r1.astype(BF16)
    lo = (r1 - mid.astype(F32)).astype(BF16)
    return hi, mid, lo


def _sum_dots(mask, pieces):
    hi, mid, lo = pieces
    d = lambda y: jnp.dot(mask, y, preferred_element_type=F32)
    return d(hi) + (d(mid) + d(lo))


def _cumsum_rows(tri, x):
    return _sum_dots(tri, _split3(x))


def _rmsnorm_kernel(x_ref, g_ref, o_ref):
    x = x_ref[...]
    y = x * lax.rsqrt(jnp.mean(x * x, axis=-1, keepdims=True) + NORM_EPS)
    o_ref[...] = (y * g_ref[...]).astype(o_ref.dtype)


def _rmsnorm(x, g, out_dtype, tm, row_off=0, nrows=None):
    d = x.shape[1]
    m = x.shape[0] if nrows is None else nrows
    rb = row_off // tm
    return pl.pallas_call(
        _rmsnorm_kernel,
        out_shape=jax.ShapeDtypeStruct((m, d), out_dtype),
        grid=(m // tm,),
        in_specs=[pl.BlockSpec((tm, d), lambda i: (rb + i, 0)), pl.BlockSpec((1, d), lambda i: (0, 0))],
        out_specs=pl.BlockSpec((tm, d), lambda i: (i, 0)),
        compiler_params=_cparams(("parallel",)),
        name="rmsnorm",
    )(x, g.reshape(1, d))


def _mm_kernel(x_ref, w_ref, *rest, nk, has_res):
    o_ref = rest[-1]
    part = jnp.dot(x_ref[...], w_ref[...], preferred_element_type=F32)
    if nk == 1:
        if has_res:
            part = part + rest[0][...]
        o_ref[...] = part.astype(o_ref.dtype)
        return
    k = pl.program_id(2)

    @pl.when(k == 0)
    def _():
        o_ref[...] = part + rest[0][...] if has_res else part

    @pl.when(k > 0)
    def _():
        o_ref[...] += part


def _matmul(x, w, *, tm, tn, tk=None, res=None, out_dtype=F32, name="matmul"):
    m, kd = x.shape
    n = w.shape[1]
    tk = kd if tk is None else tk
    nk = kd // tk
    assert m % tm == 0 and n % tn == 0 and kd % tk == 0
    assert nk == 1 or out_dtype == F32
    in_specs = [pl.BlockSpec((tm, tk), lambda i, j, k: (i, k)), pl.BlockSpec((tk, tn), lambda i, j, k: (k, j))]
    args = [x, w]
    if res is not None:
        in_specs.append(pl.BlockSpec((tm, tn), lambda i, j, k: (i, j)))
        args.append(res)
    return pl.pallas_call(
        functools.partial(_mm_kernel, nk=nk, has_res=res is not None),
        out_shape=jax.ShapeDtypeStruct((m, n), out_dtype),
        grid=(m // tm, n // tn, nk),
        in_specs=in_specs,
        out_specs=pl.BlockSpec((tm, tn), lambda i, j, k: (i, j)),
        compiler_params=_cparams(("parallel", "parallel", "arbitrary")),
        name=name,
    )(*args)


def _glu_kernel(x_ref, wg_ref, wu_ref, o_ref):
    x = x_ref[...]
    a = jnp.dot(x, wg_ref[...], preferred_element_type=F32)
    b = jnp.dot(x, wu_ref[...], preferred_element_type=F32)
    o_ref[...] = ((a * _sigmoid(a)) * b).astype(o_ref.dtype)


def _glu(x, wg, wu, *, tm, tn):
    m, kd = x.shape
    n = wg.shape[1]
    return pl.pallas_call(
        _glu_kernel,
        out_shape=jax.ShapeDtypeStruct((m, n), BF16),
        grid=(m // tm, n // tn),
        in_specs=[pl.BlockSpec((tm, kd), lambda i, j: (i, 0)),
                  pl.BlockSpec((kd, tn), lambda i, j: (0, j)),
                  pl.BlockSpec((kd, tn), lambda i, j: (0, j))],
        out_specs=pl.BlockSpec((tm, tn), lambda i, j: (i, j)),
        compiler_params=_cparams(("parallel", "parallel")),
        name="ffn_glu",
    )(x, wg, wu)


def _gate_merge_kernel(h_ref, g0_ref, g1_ref, g2_ref, b0_ref, b1_ref, b2_ref, wb_ref, o_ref):
    h = h_ref[...]
    acc = None
    for n, (g_ref, b_ref) in enumerate(((g0_ref, b0_ref), (g1_ref, b1_ref), (g2_ref, b2_ref))):
        gate = _sigmoid(jnp.dot(h, g_ref[...], preferred_element_type=F32))
        term = gate * jnp.dot(b_ref[...], wb_ref[n], preferred_element_type=F32)
        acc = term if acc is None else acc + term
    o_ref[...] = acc.astype(o_ref.dtype)


def _gate_merge(h, w_gate, branches, w_branch, *, tm, tn):
    m, d = h.shape
    nj = d // tn
    w = branches[0].shape[1]
    gate_spec = lambda n: pl.BlockSpec((d, tn), lambda i, j, n=n: (0, n * nj + j))
    br_spec = pl.BlockSpec((tm, w), lambda i, j: (i, 0))
    return pl.pallas_call(
        _gate_merge_kernel,
        out_shape=jax.ShapeDtypeStruct((m, d), BF16),
        grid=(m // tm, nj),
        in_specs=[pl.BlockSpec((tm, d), lambda i, j: (i, 0)), gate_spec(0), gate_spec(1), gate_spec(2),
                  br_spec, br_spec, br_spec, pl.BlockSpec((3, w, tn), lambda i, j: (0, 0, j))],
        out_specs=pl.BlockSpec((tm, tn), lambda i, j: (i, j)),
        compiler_params=_cparams(("parallel", "parallel")),
        name="gate_merge",
    )(h, w_gate, w_gate, w_gate, branches[0], branches[1], branches[2], w_branch)


def _hgrn_kernel(q_ref, f_ref, i_ref, g_ref, lb_ref, nw_ref, s0_ref, y_ref, so_ref, st_scr, *, blk, nblk, heads):
    tc = pl.program_id(2)

    @pl.when(tc == 0)
    def _():
        for h in range(heads):
            st_scr[h] = s0_ref[0, h].T

    rows = blk * nblk
    row = lax.broadcasted_iota(jnp.int32, (rows, rows), 0)
    col = lax.broadcasted_iota(jnp.int32, (rows, rows), 1)
    in_block = (row - col).astype(jnp.uint32) <= (row & (blk - 1)).astype(jnp.uint32)
    same_block = (row & -blk) == (col & -blk)
    tri = jnp.where(in_block, 1.0, 0.0).astype(BF16)
    ones_bd = jnp.where(same_block, 1.0, 0.0).astype(BF16)
    lb = lb_ref[...]
    q = q_ref[...]
    f = lb + (1.0 - lb) * _sigmoid(f_ref[...])
    log_f = jnp.log(f)
    k = 1.0 - f
    iv = i_ref[...]
    g = g_ref[...]
    pieces = _split3(log_f)
    b = _sum_dots(tri, pieces)
    total = _sum_dots(ones_bd, pieces)
    qe = (q * _sigmoid(q)) * jnp.exp(b)
    ke = k * jnp.exp(-b)
    k_end = k * jnp.exp(total - b)
    decay = jnp.exp(total)
    lanes = [slice(h * HG_DK, (h + 1) * HG_DK) for h in range(heads)]
    blocks = [slice(n * blk, (n + 1) * blk) for n in range(nblk)]
    att = [jnp.where(in_block, _dot1(qe[:, hs], ke[:, hs], _NT), 0.0) for hs in lanes]
    o_intra = [_dot1(a, iv[:, hs], _NN) for a, hs in zip(att, lanes)]
    grams = [[_dot1(iv[sl, hs], k_end[sl, hs], _TN) for sl in blocks] for hs in lanes]
    o_heads = []
    for h, hs in enumerate(lanes):
        states = [st_scr[h]]
        for n, sl in enumerate(blocks):
            states.append(states[n] * decay[sl.start:sl.start + 1, hs] + grams[h][n])
        st_scr[h] = states[nblk]
        o_inter = [_dot1(qe[sl, hs], states[n], _NT) for n, sl in enumerate(blocks)]
        o = o_intra[h] + jnp.concatenate(o_inter, axis=0)
        o_heads.append(o * lax.rsqrt(jnp.mean(o * o, axis=-1, keepdims=True) + NORM_EPS))
    y = jnp.concatenate(o_heads, axis=1) * nw_ref[...]
    y_ref[...] = (y * (g * _sigmoid(g))).astype(y_ref.dtype)

    @pl.when(tc == pl.num_programs(2) - 1)
    def _():
        for h in range(heads):
            so_ref[0, h] = st_scr[h].T


def _hgrn(p_hg, lb, norm_w, s0, *, batch, seq, row_off, tc):
    nt = seq // tc
    rb = row_off // tc
    hg = HG_HEAD_GROUP
    wcols = hg * HG_DV
    sec = BR_WIDTH // wcols
    spec = lambda s: pl.BlockSpec((tc, wcols), lambda b, h, t, s=s: (rb + b * nt + t, s * sec + h))
    vec = pl.BlockSpec((1, wcols), lambda b, h, t: (0, h))
    st_spec = pl.BlockSpec((1, hg, HG_DK, HG_DV), lambda b, h, t: (b, h, 0, 0))
    return pl.pallas_call(
        functools.partial(_hgrn_kernel, blk=HG_BLOCK, nblk=tc // HG_BLOCK, heads=hg),
        out_shape=(jax.ShapeDtypeStruct((batch * seq, BR_WIDTH), BF16),
                   jax.ShapeDtypeStruct((batch, HG_HEADS, HG_DK, HG_DV), F32)),
        grid=(batch, HG_HEADS // hg, nt),
        in_specs=[spec(0), spec(1), spec(2), spec(3), vec, vec, st_spec],
        out_specs=(pl.BlockSpec((tc, wcols), lambda b, h, t: (b * nt + t, h)), st_spec),
        scratch_shapes=[pltpu.VMEM((hg, HG_DV, HG_DK), F32)],
        compiler_params=_cparams(("parallel", "parallel", "arbitrary")),
        name="hgrn2",
    )(p_hg, p_hg, p_hg, p_hg, lb.reshape(1, BR_WIDTH), norm_w.reshape(1, BR_WIDTH), s0)


def _rwkv_kernel(r_ref, k_ref, v_ref, lo_ref, pr_ref, pk_ref, pv_ref, plo_ref, mur_ref, muk_ref, muv_ref, mulo_ref,
                 w0_ref, a0_ref, kk_ref, ka_ref, rk_ref, lnw_ref, lnb_ref, w2_ref, a2_ref, g2_ref, s0_ref,
                 y_ref, so_ref, s_scr, prev_r, prev_k, prev_v, prev_lo, *, chunk, nsub, heads):
    tc = pl.program_id(2)
    hd = RW_HD

    @pl.when(tc == 0)
    def _():
        s_scr[...] = s0_ref[0]
        prev_r[...] = pr_ref[0]
        prev_k[...] = pk_ref[0]
        prev_v[...] = pv_ref[0]
        prev_lo[...] = plo_ref[0]

    rows = chunk * nsub

    def token_shift(x_ref, prev_scr, mu_ref):
        x = x_ref[...]
        rolled = pltpu.roll(x, 1, 0)
        first = lax.broadcasted_iota(jnp.int32, x.shape, 0) == 0
        shifted = jnp.where(first, prev_scr[...], rolled)
        prev_scr[...] = x[rows - 1:rows, :]
        return x + (shifted - x) * mu_ref[...]

    r = token_shift(r_ref, prev_r, mur_ref)
    k = token_shift(k_ref, prev_k, muk_ref)
    v = token_shift(v_ref, prev_v, muv_ref)
    lo = token_shift(lo_ref, prev_lo, mulo_ref)
    wd = lo[:, :RW_W_RANK]
    ad = lo[:, RW_W_RANK:RW_W_RANK + RW_A_RANK]
    gd = lo[:, RW_W_RANK + RW_A_RANK:]

    wl = w0_ref[...] + _dot3(jnp.tanh(wd), w2_ref[...], _NN)
    neg = -wl
    softplus = jnp.maximum(neg, 0.0) + jnp.log(1.0 + jnp.exp(-jnp.abs(neg)))
    log_decay = -jnp.exp(-softplus - 0.5)
    lr = _sigmoid(a0_ref[...] + _dot3(ad, a2_ref[...], _NN))
    gate = _dot3(_sigmoid(gd), g2_ref[...], _NN)
    kk_raw = k * kk_ref[...]
    k_mod = k * (1.0 + (lr - 1.0) * ka_ref[...])
    rk_w = r * k_mod * rk_ref[...]

    row = lax.broadcasted_iota(jnp.int32, (chunk, chunk), 0)
    col = lax.broadcasted_iota(jnp.int32, (chunk, chunk), 1)
    lower = row >= col
    strict = row > col
    tri = jnp.where(lower, 1.0, 0.0).astype(BF16)
    eye = jnp.where(row == col, 1.0, 0.0)
    row2 = lax.broadcasted_iota(jnp.int32, (chunk, 2 * chunk), 0)
    col2 = lax.broadcasted_iota(jnp.int32, (chunk, 2 * chunk), 1) & (chunk - 1)
    lower2 = row2 >= col2
    strict2 = row2 > col2

    chains = [(s, j) for s in range(nsub) for j in range(heads)]
    e_last = {}
    ops = {}
    for s in range(nsub):
        rs = slice(s * chunk, (s + 1) * chunk)
        c = _cumsum_rows(tri, log_decay[rs, :])
        c_last = c[chunk - 1:chunk, :]
        e_c = jnp.exp(c)
        e_cprev = jnp.exp(c - log_decay[rs, :])
        e_inv = jnp.exp(-c)
        e_end = jnp.exp(c_last - c)
        e_last[s] = jnp.exp(c_last)
        for j in range(heads):
            cs = slice(j * hd, (j + 1) * hd)
            kkj = kk_raw[rs, cs]
            kkj = kkj / jnp.maximum(jnp.sqrt(jnp.sum(kkj * kkj, axis=-1, keepdims=True)), 1e-12)
            bj = kkj * lr[rs, cs]
            kj = k_mod[rs, cs]
            ops[s, j] = dict(
                a_dec=-kkj * e_cprev[:, cs], r_dec=r[rs, cs] * e_c[:, cs], v=v[rs, cs],
                bk_inv=jnp.concatenate([bj * e_inv[:, cs], kj * e_inv[:, cs]], axis=0),
                b_end=bj * e_end[:, cs], k_end=kj * e_end[:, cs])

    pair = {c: _dot1(jnp.concatenate([ops[c]["a_dec"], ops[c]["r_dec"]], axis=0), ops[c]["bk_inv"], _NT)
            for c in chains}
    m_abk = {c: jnp.where(strict2, pair[c][:chunk, :], 0.0) for c in chains}
    n_rbk = {c: jnp.where(lower2, pair[c][chunk:, :], 0.0) for c in chains}
    mv = {c: _dot1(m_abk[c][:, chunk:], ops[c]["v"], _NN) for c in chains}
    pw = {c: m_abk[c][:, :chunk] for c in chains}
    inv = {c: eye + pw[c] for c in chains}
    for _ in range(chunk.bit_length() - 2):
        pw = {c: _dot1(pw[c], pw[c], _NN) for c in chains}
        inv = {c: inv[c] + _dot1(inv[c], pw[c], _NN) for c in chains}
    t_amv = {c: _dot1(inv[c], jnp.concatenate([ops[c]["a_dec"], mv[c]], axis=1), _NN) for c in chains}
    nr = {c: _dot1(n_rbk[c][:, :chunk], t_amv[c], _NN) for c in chains}
    y_const = {c: nr[c][:, hd:] + _dot1(n_rbk[c][:, chunk:], ops[c]["v"], _NN) for c in chains}
    r_eff = {c: ops[c]["r_dec"] + nr[c][:, :hd] for c in chains}
    gram = {c: _dot1(t_amv[c], ops[c]["b_end"], _TN) for c in chains}
    s_const = {c: gram[c][hd:, :] + _dot1(ops[c]["v"], ops[c]["k_end"], _TN) for c in chains}

    y_heads = {}
    for j in range(heads):
        cs = slice(j * hd, (j + 1) * hd)
        st = s_scr[j]
        for s in range(nsub):
            y_heads[s, j] = _dot1(r_eff[s, j], st, _NT) + y_const[s, j]
            st = st * e_last[s][:, cs] + (_dot1(st, gram[s, j][:hd, :], _NN) + s_const[s, j])
        s_scr[j] = st

    for s in range(nsub):
        rs = slice(s * chunk, (s + 1) * chunk)
        outs = []
        for j in range(heads):
            cs = slice(j * hd, (j + 1) * hd)
            y = y_heads[s, j]
            mean = jnp.mean(y, axis=-1, keepdims=True)
            var = jnp.mean(jnp.square(y - mean), axis=-1, keepdims=True)
            yn = (y - mean) * lax.rsqrt(var + RW_LN_EPS) * lnw_ref[:, cs] + lnb_ref[:, cs]
            yn = yn + jnp.sum(rk_w[rs, cs], axis=-1, keepdims=True) * ops[s, j]["v"]
            outs.append(yn * gate[rs, cs])
        y_ref[rs, :] = jnp.concatenate(outs, axis=1).astype(y_ref.dtype)

    @pl.when(tc == pl.num_programs(2) - 1)
    def _():
        so_ref[0] = s_scr[...]


def _rwkv(p_rw, prev, s0, prm, *, batch, seq, row_off, nsub):
    hg = RW_HEAD_GROUP
    wcols = hg * RW_HD
    rows = RW_CHUNK * nsub
    nt = seq // rows
    rb = row_off // rows
    nsec = BR_WIDTH // wcols
    lo_blk = 3 * nsec
    assert RW_IN - 3 * BR_WIDTH == wcols
    tok = lambda s: pl.BlockSpec((rows, wcols), lambda b, g, t, s=s: (rb + b * nt + t, s * nsec + g))
    tok_lo = pl.BlockSpec((rows, wcols), lambda b, g, t: (rb + b * nt + t, lo_blk))
    prv = lambda s: pl.BlockSpec((1, 1, wcols), lambda b, g, t, s=s: (b, 0, s * nsec + g))
    prv_lo = pl.BlockSpec((1, 1, wcols), lambda b, g, t: (b, 0, lo_blk))
    mu = lambda s: pl.BlockSpec((1, wcols), lambda b, g, t, s=s: (0, s * nsec + g))
    mu_lo = pl.BlockSpec((1, wcols), lambda b, g, t: (0, lo_blk))
    vec = pl.BlockSpec((1, wcols), lambda b, g, t: (0, g))
    lora = lambda rank: pl.BlockSpec((rank, wcols), lambda b, g, t: (0, g))
    st_spec = pl.BlockSpec((1, hg, RW_HD, RW_HD), lambda b, g, t: (b, g, 0, 0))
    prev3 = prev.reshape(batch, 1, RW_IN)
    mu2 = prm["mu"].reshape(1, RW_IN)
    v1 = lambda a: a.reshape(1, BR_WIDTH)
    return pl.pallas_call(
        functools.partial(_rwkv_kernel, chunk=RW_CHUNK, nsub=nsub, heads=hg),
        out_shape=(jax.ShapeDtypeStruct((batch * seq, BR_WIDTH), BF16),
                   jax.ShapeDtypeStruct((batch, RW_HEADS, RW_HD, RW_HD), F32)),
        grid=(batch, RW_HEADS // hg, nt),
        in_specs=[tok(0), tok(1), tok(2), tok_lo, prv(0), prv(1), prv(2), prv_lo, mu(0), mu(1), mu(2), mu_lo,
                  vec, vec, vec, vec, vec, vec, vec, lora(RW_W_RANK), lora(RW_A_RANK), lora(RW_G_RANK), st_spec],
        out_specs=(pl.BlockSpec((rows, wcols), lambda b, g, t: (b * nt + t, g)), st_spec),
        scratch_shapes=[pltpu.VMEM((hg, RW_HD, RW_HD), F32)] + [pltpu.VMEM((1, wcols), F32)] * 4,
        compiler_params=_cparams(("parallel", "parallel", "arbitrary")),
        name="rwkv7",
    )(p_rw, p_rw, p_rw, p_rw, prev3, prev3, prev3, prev3, mu2, mu2, mu2, mu2,
      v1(prm["w0"]), v1(prm["a0"]), v1(prm["kk"]), v1(prm["ka"]), v1(prm["rk"]), v1(prm["lnw"]), v1(prm["lnb"]),
      prm["w2"], prm["a2"], prm["g2"], s0)


def _rope_partner(x):
    width = x.shape[-1]
    half = SW_HD // 2
    lane = lax.broadcasted_iota(jnp.int32, x.shape, x.ndim - 1)
    return jnp.where(lane % SW_HD < half, pltpu.roll(x, width - half, x.ndim - 1), pltpu.roll(x, half, x.ndim - 1))


def _swa_kernel(sink_ref, q_ref, kv_ref, cos_ref, sin_ref, past_ref, o_ref, kvo_ref, buf, *, past_valid):
    c = pl.program_id(1)
    kvw = SW_KV_HEADS * SW_HD

    @pl.when(c == 0)
    def _():
        buf[0:WINDOW, :] = past_ref[0]

    @pl.when(c > 0)
    def _():
        buf[0:WINDOW, :] = buf[CHUNK:CHUNK + WINDOW, :]

    cos = cos_ref[...]
    sin = sin_ref[...]
    kv = kv_ref[...]
    k = kv[:, :kvw]
    v = kv[:, kvw:]
    k_rot = k * cos + _rope_partner(k) * sin
    buf[WINDOW:WINDOW + CHUNK, :kvw] = k_rot
    buf[WINDOW:WINDOW + CHUNK, kvw:] = v
    kvo_ref[:, :kvw] = k_rot
    kvo_ref[:, kvw:] = v

    q = q_ref[...]
    reps = q.shape[1] // cos.shape[1]
    cos_q = jnp.concatenate([cos] * reps, axis=1)
    sin_q = jnp.concatenate([sin] * reps, axis=1)
    q_rot = q * cos_q + _rope_partner(q) * sin_q

    band = WINDOW + CHUNK
    part_rows = SW_PART_HEADS * CHUNK
    slot = lax.broadcasted_iota(jnp.int32, (part_rows, band), 1)
    valid = None if past_valid else slot >= (WINDOW // CHUNK - c) * CHUNK
    kgs = [buf[:, g * SW_HD:(g + 1) * SW_HD].astype(BF16) for g in range(SW_KV_HEADS)]
    vgs = [buf[:, kvw + g * SW_HD:kvw + (g + 1) * SW_HD].astype(BF16) for g in range(SW_KV_HEADS)]
    parts = [range(h0, h0 + SW_PART_HEADS) for h0 in range(0, SW_HEADS, SW_PART_HEADS)]
    qs = [jnp.concatenate([q_rot[:, h * SW_HD:(h + 1) * SW_HD] for h in hs], axis=0).astype(BF16) for hs in parts]
    sinks = [jnp.concatenate([jnp.full((CHUNK, 1), sink_ref[h], F32) for h in hs], axis=0) for hs in parts]
    ss = [lax.dot_general(qp, kgs[hs[0] // Q_PER_KV], (_NT, ((), ())), preferred_element_type=F32) * (SW_HD ** -0.5)
          for qp, hs in zip(qs, parts)]
    if valid is not None:
        ss = [jnp.where(valid, s, MASK_VALUE) for s in ss]
    ms = [jnp.maximum(jnp.max(s, axis=-1, keepdims=True), sink) for s, sink in zip(ss, sinks)]
    es = [jnp.exp(s - m) for s, m in zip(ss, ms)]
    dens = [jnp.sum(e, axis=-1, keepdims=True) + jnp.exp(sink - m) for e, sink, m in zip(es, sinks, ms)]
    ogs = [jnp.dot(e.astype(BF16), vgs[hs[0] // Q_PER_KV], preferred_element_type=F32) / den
           for e, hs, den in zip(es, parts, dens)]
    outs = [og[i * CHUNK:(i + 1) * CHUNK, :] for og in ogs for i in range(SW_PART_HEADS)]
    o_ref[...] = jnp.concatenate(outs, axis=1).astype(o_ref.dtype)


def _swa(p_sw, cos_tab, sin_tab, past_kv, sinks, *, batch, seq, row_off, past_valid):
    nc = seq // CHUNK
    rb = row_off // CHUNK
    qw = SW_HEADS * SW_HD
    kvw2 = 2 * SW_KV_HEADS * SW_HD
    grid_spec = pltpu.PrefetchScalarGridSpec(
        num_scalar_prefetch=1,
        grid=(batch, nc),
        in_specs=[pl.BlockSpec((CHUNK, qw), lambda b, c, s: (rb + b * nc + c, 0)),
                  pl.BlockSpec((CHUNK, kvw2), lambda b, c, s: (rb + b * nc + c, qw // kvw2)),
                  pl.BlockSpec((CHUNK, LANES), lambda b, c, s: (c, 0)),
                  pl.BlockSpec((CHUNK, LANES), lambda b, c, s: (c, 0)),
                  pl.BlockSpec((1, WINDOW, kvw2), lambda b, c, s: (b, 0, 0))],
        out_specs=(pl.BlockSpec((CHUNK, qw), lambda b, c, s: (b * nc + c, 0)),
                   pl.BlockSpec((CHUNK, kvw2), lambda b, c, s: (b * nc + c, 0))),
        scratch_shapes=[pltpu.VMEM((WINDOW + CHUNK, kvw2), F32)],
    )
    return pl.pallas_call(
        functools.partial(_swa_kernel, past_valid=past_valid),
        out_shape=(jax.ShapeDtypeStruct((batch * seq, qw), BF16),
                   jax.ShapeDtypeStruct((batch * seq, kvw2), F32)),
        grid_spec=grid_spec,
        compiler_params=_cparams(("parallel", "arbitrary")),
        name="swa",
    )(sinks, p_sw, p_sw, cos_tab, sin_tab, past_kv)


def _rope_tables(pos):
    half = SW_HD // 2
    inv = ROPE_THETA ** (-jnp.arange(half, dtype=F32) / half)
    ang = pos.astype(F32)[:, None] * inv[None, :]
    cos, sin = jnp.cos(ang), jnp.sin(ang)
    reps = LANES // SW_HD
    return jnp.tile(jnp.concatenate([cos, cos], axis=1), (1, reps)), jnp.tile(jnp.concatenate([-sin, sin], axis=1), (1, reps))


def _norm_router_kernel(x_ref, g_ref, wr_ref, h_ref, route_ref):
    x = x_ref[...]
    y = x * lax.rsqrt(jnp.mean(x * x, axis=-1, keepdims=True) + NORM_EPS)
    h = y * g_ref[...]
    h_ref[...] = h
    logits = _dot3(h, wr_ref[...], _NN)
    lane = lax.broadcasted_iota(jnp.int32, logits.shape, 1).astype(F32)
    neg_inf = -jnp.inf
    logits = jnp.where(lane < N_EXPERTS, logits, neg_inf)
    m1 = jnp.max(logits, axis=-1, keepdims=True)
    i1 = jnp.min(jnp.where(logits == m1, lane, float(LANES)), axis=-1, keepdims=True)
    rest = jnp.where(lane == i1, neg_inf, logits)
    m2 = jnp.max(rest, axis=-1, keepdims=True)
    i2 = jnp.min(jnp.where(rest == m2, lane, float(LANES)), axis=-1, keepdims=True)
    e2 = jnp.exp(m2 - m1)
    g1 = 1.0 / (1.0 + e2)
    g2 = e2 / (1.0 + e2)
    route = jnp.where(lane == 0, i1, jnp.where(lane == 1, i2, jnp.where(lane == 2, g1, jnp.where(lane == 3, g2, 0.0))))
    route_ref[...] = route


def _norm_router(x, g, router, *, tm):
    m, d = x.shape
    wr = jnp.zeros((d, LANES), F32).at[:, :N_EXPERTS].set(router)
    return pl.pallas_call(
        _norm_router_kernel,
        out_shape=(jax.ShapeDtypeStruct((m, d), F32), jax.ShapeDtypeStruct((m, LANES), F32)),
        grid=(m // tm,),
        in_specs=[pl.BlockSpec((tm, d), lambda i: (i, 0)), pl.BlockSpec((1, d), lambda i: (0, 0)),
                  pl.BlockSpec((d, LANES), lambda i: (0, 0))],
        out_specs=(pl.BlockSpec((tm, d), lambda i: (i, 0)), pl.BlockSpec((tm, LANES), lambda i: (i, 0))),
        compiler_params=_cparams(("parallel",)),
        name="norm_router",
    )(x, g.reshape(1, d), wr)


def _gather_rows_kernel(idx_ref, nxt_ref, src_hbm, o_ref, buf, sem, *, rows):
    i = pl.program_id(0)
    slot = i % 2

    def row_copy(src_row, s, r):
        return pltpu.make_async_copy(src_hbm.at[pl.ds(src_row, 1)], buf.at[s, pl.ds(r, 1)], sem.at[s])

    def issue(ref, s):
        def start(r, carry):
            row_copy(ref[0, 0, r], s, r).start()
            return carry

        lax.fori_loop(0, rows, start, 0, unroll=8)

    @pl.when(i == 0)
    def _():
        issue(idx_ref, 0)

    @pl.when(i + 1 < pl.num_programs(0))
    def _():
        issue(nxt_ref, 1 - slot)

    def wait(r, carry):
        row_copy(0, slot, r).wait()
        return carry

    lax.fori_loop(0, rows, wait, 0, unroll=8)
    o_ref[...] = buf[slot].astype(o_ref.dtype)


def _gather_rows(src, idx, out_dtype, *, rows):
    n = idx.shape[0]
    nb = n // rows
    d = src.shape[1]
    return pl.pallas_call(
        functools.partial(_gather_rows_kernel, rows=rows),
        out_shape=jax.ShapeDtypeStruct((n, d), out_dtype),
        grid=(nb,),
        in_specs=[pl.BlockSpec((1, 1, rows), lambda i: (i, 0, 0), memory_space=pltpu.SMEM),
                  pl.BlockSpec((1, 1, rows), lambda i: (jnp.minimum(i + 1, nb - 1), 0, 0), memory_space=pltpu.SMEM),
                  pl.BlockSpec(memory_space=pl.ANY)],
        out_specs=pl.BlockSpec((rows, d), lambda i: (i, 0)),
        scratch_shapes=[pltpu.VMEM((2, rows, d), src.dtype), pltpu.SemaphoreType.DMA((2,))],
        compiler_params=_cparams(("arbitrary",)),
        name="moe_gather",
    )(idx.reshape(nb, 1, rows), idx.reshape(nb, 1, rows), src)


def _moe_glu_kernel(be_ref, nv_ref, x_ref, wg_ref, wu_ref, o_ref):
    used = pl.program_id(1) < nv_ref[0]

    @pl.when(used)
    def _():
        x = x_ref[...]
        a = jnp.dot(x, wg_ref[0].astype(BF16), preferred_element_type=F32)
        b = jnp.dot(x, wu_ref[0].astype(BF16), preferred_element_type=F32)
        o_ref[...] = ((a * _sigmoid(a)) * b).astype(o_ref.dtype)

    @pl.when(jnp.logical_not(used))
    def _():
        o_ref[...] = jnp.zeros_like(o_ref)


def _moe_glu(rows_x, wg, wu, block_e, n_valid, *, tn):
    n, d = rows_x.shape
    nb = n // MOE_ROWS
    f = wg.shape[2]
    blk = lambda b, nv: jnp.minimum(b, nv[0] - 1)
    grid_spec = pltpu.PrefetchScalarGridSpec(
        num_scalar_prefetch=2,
        grid=(f // tn, nb),
        in_specs=[pl.BlockSpec((MOE_ROWS, d), lambda j, b, be, nv: (blk(b, nv), 0)),
                  pl.BlockSpec((1, d, tn), lambda j, b, be, nv: (be[blk(b, nv)], 0, j)),
                  pl.BlockSpec((1, d, tn), lambda j, b, be, nv: (be[blk(b, nv)], 0, j))],
        out_specs=pl.BlockSpec((MOE_ROWS, tn), lambda j, b, be, nv: (b, j)),
    )
    return pl.pallas_call(
        _moe_glu_kernel,
        out_shape=jax.ShapeDtypeStruct((n, f), BF16),
        grid_spec=grid_spec,
        compiler_params=_cparams(("arbitrary", "arbitrary")),
        name="moe_glu",
    )(block_e, n_valid, rows_x, wg, wu)


def _moe_down_kernel(be_ref, nv_ref, x_ref, w_ref, o_ref):
    k = pl.program_id(2)

    used = pl.program_id(1) < nv_ref[0]

    @pl.when(used)
    def _():
        part = jnp.dot(x_ref[...], w_ref[0], preferred_element_type=F32)

        @pl.when(k == 0)
        def _():
            o_ref[...] = part

        @pl.when(k > 0)
        def _():
            o_ref[...] += part

    @pl.when(jnp.logical_not(used))
    def _():
        o_ref[...] = jnp.zeros_like(o_ref)


def _moe_down(act, wd, block_e, n_valid, *, tn, tk):
    n, f = act.shape
    nb = n // MOE_ROWS
    d = wd.shape[2]
    blk = lambda b, nv: jnp.minimum(b, nv[0] - 1)
    grid_spec = pltpu.PrefetchScalarGridSpec(
        num_scalar_prefetch=2,
        grid=(d // tn, nb, f // tk),
        in_specs=[pl.BlockSpec((MOE_ROWS, tk), lambda j, b, k, be, nv: (blk(b, nv), k)),
                  pl.BlockSpec((1, tk, tn), lambda j, b, k, be, nv: (be[blk(b, nv)], k, j))],
        out_specs=pl.BlockSpec((MOE_ROWS, tn), lambda j, b, k, be, nv: (b, j)),
    )
    return pl.pallas_call(
        _moe_down_kernel,
        out_shape=jax.ShapeDtypeStruct((n, d), F32),
        grid_spec=grid_spec,
        compiler_params=_cparams(("arbitrary", "arbitrary", "arbitrary")),
        name="moe_down",
    )(block_e, n_valid, act, wd)


def _combine_kernel(p0_ref, p1_ref, n0_ref, n1_ref, x_ref, gate_ref, out_hbm, o_ref, buf, sem, *, rows):
    i = pl.program_id(0)
    slot = i % 2

    def row_copy(src_row, s, k, r):
        return pltpu.make_async_copy(out_hbm.at[pl.ds(src_row, 1)], buf.at[s, k, pl.ds(r, 1)], sem.at[s, k])

    def issue(ref0, ref1, s):
        def start(r, carry):
            row_copy(ref0[0, 0, r], s, 0, r).start()
            row_copy(ref1[0, 0, r], s, 1, r).start()
            return carry

        lax.fori_loop(0, rows, start, 0, unroll=8)

    @pl.when(i == 0)
    def _():
        issue(p0_ref, p1_ref, 0)

    @pl.when(i + 1 < pl.num_programs(0))
    def _():
        issue(n0_ref, n1_ref, 1 - slot)

    def wait(r, carry):
        row_copy(0, slot, 0, r).wait()
        row_copy(0, slot, 1, r).wait()
        return carry

    lax.fori_loop(0, rows, wait, 0, unroll=8)
    gate = gate_ref[...]
    g0 = gate[:, 2:3]
    g1 = gate[:, 3:4]
    o_ref[...] = x_ref[...] + (buf[slot, 0] * g0 + buf[slot, 1] * g1)


def _combine(x, route, out_rows, pos0, pos1, *, rows):
    m, d = x.shape
    nb = m // rows
    idx_spec = pl.BlockSpec((1, 1, rows), lambda i: (i, 0, 0), memory_space=pltpu.SMEM)
    nxt_spec = pl.BlockSpec((1, 1, rows), lambda i: (jnp.minimum(i + 1, nb - 1), 0, 0), memory_space=pltpu.SMEM)
    p0 = pos0.reshape(nb, 1, rows)
    p1 = pos1.reshape(nb, 1, rows)
    return pl.pallas_call(
        functools.partial(_combine_kernel, rows=rows),
        out_shape=jax.ShapeDtypeStruct((m, d), F32),
        grid=(nb,),
        in_specs=[idx_spec, idx_spec, nxt_spec, nxt_spec, pl.BlockSpec((rows, d), lambda i: (i, 0)),
                  pl.BlockSpec((rows, LANES), lambda i: (i, 0)), pl.BlockSpec(memory_space=pl.ANY)],
        out_specs=pl.BlockSpec((rows, d), lambda i: (i, 0)),
        scratch_shapes=[pltpu.VMEM((2, TOP_K, rows, d), F32), pltpu.SemaphoreType.DMA((2, TOP_K))],
        compiler_params=_cparams(("arbitrary",)),
        name="moe_combine",
    )(p0, p1, p0, p1, x, route, out_rows)


def _moe_layer(x, g, router, wg, wu, wd, *, tm, glu_tn, down_tn, down_tk):
    m, d = x.shape
    h, route = _norm_router(x, g, router, tm=tm)
    eid = route[:, :TOP_K].astype(jnp.int32).reshape(-1)
    n_assign = m * TOP_K
    onehot = (eid[:, None] == jnp.arange(N_EXPERTS)[None, :]).astype(jnp.int32)
    rank = jnp.take_along_axis(jnp.cumsum(onehot, axis=0), eid[:, None], axis=1)[:, 0] - 1
    counts = jnp.sum(onehot, axis=0)
    padded = (counts + MOE_ROWS - 1) // MOE_ROWS * MOE_ROWS
    pend = jnp.cumsum(padded)
    pstart = pend - padded
    dest = pstart[eid] + rank
    nb = -(-n_assign // MOE_ROWS) + N_EXPERTS
    tok = jnp.arange(n_assign, dtype=jnp.int32) // TOP_K
    src_tok = jnp.zeros((nb * MOE_ROWS,), jnp.int32).at[dest].set(tok)
    block_e = jnp.minimum(jnp.searchsorted(pend, jnp.arange(nb) * MOE_ROWS, side="right"), N_EXPERTS - 1)
    block_e = block_e.astype(jnp.int32)
    n_valid = (pend[-1] // MOE_ROWS).astype(jnp.int32).reshape(1)
    rows_x = _gather_rows(h, src_tok, BF16, rows=MOE_ROWS)
    act = _moe_glu(rows_x, wg, wu, block_e, n_valid, tn=glu_tn)
    out_rows = _moe_down(act, wd, block_e, n_valid, tn=down_tn, tk=down_tk)
    dest2 = dest.reshape(m, TOP_K).astype(jnp.int32)
    return _combine(x, route, out_rows, dest2[:, 0], dest2[:, 1], rows=256)


def kernel(x_prompt, x_sample, state_hgrn, state_rwkv, state_rwkv_shift, cache_swa_k, cache_swa_v, norm_mix, w_in, hg_lower_bounds, hg_norm, rw_mu, rw_w0, rw_w2, rw_a0, rw_a2, rw_g2, rw_kk, rw_ka, rw_rk, rw_lnw, rw_lnb, sw_sinks, w_branch, w_gate_br, w_out, norm_ffn, ffn_w_gate, ffn_w_up, ffn_w_down, moe_router, moe_w_gate, moe_w_up, moe_w_down, norm_final):
    bp, tp, d = x_prompt.shape
    bs, ts, _ = x_sample.shape
    mp, ms = bp * tp, bs * ts
    m = mp + ms
    depth = w_in.shape[0]
    tm = 768
    tm_norm = 256
    x = jnp.concatenate([x_prompt.reshape(mp, d), x_sample.reshape(ms, d)], axis=0)

    lbs = jax.nn.softmax(hg_lower_bounds.astype(F32), axis=0)
    lbs = jnp.cumsum(lbs, axis=0) - lbs[0]
    cos_p, sin_p = _rope_tables(jnp.arange(tp))
    cos_s, sin_s = _rope_tables(PAST_LEN + jnp.arange(ts))
    kvw = SW_KV_HEADS * SW_HD

    outs = {k: [] for k in ("hg_p", "hg_s", "rw_p", "rw_s", "sh_p", "sh_s", "k_p", "v_p", "k_s", "v_s")}
    for l in range(depth):
        h = _rmsnorm(x, norm_mix[l], BF16, tm_norm)
        w_in_l = w_in[l]
        p_hg = _matmul(h, w_in_l[:, :HG_IN].astype(BF16), tm=tm, tn=1024, name="proj_hg")
        p_rw = _matmul(h, w_in_l[:, HG_IN:HG_IN + RW_IN].astype(BF16), tm=tm, tn=RW_IN // 2, name="proj_rw")
        p_sw = _matmul(h, w_in_l[:, HG_IN + RW_IN:].astype(BF16), tm=tm, tn=SW_IN, name="proj_sw")

        y_hg_p, s_hg_p = _hgrn(p_hg, lbs[l], hg_norm[l], jnp.zeros((bp, HG_HEADS, HG_DK, HG_DV), F32),
                               batch=bp, seq=tp, row_off=0, tc=256)
        y_hg_s, s_hg_s = _hgrn(p_hg, lbs[l], hg_norm[l], state_hgrn[l], batch=bs, seq=ts, row_off=mp, tc=ts)
        rw_prm = dict(mu=rw_mu[l], w0=rw_w0[l], w2=rw_w2[l], a0=rw_a0[l], a2=rw_a2[l], g2=rw_g2[l], kk=rw_kk[l],
                      ka=rw_ka[l], rk=rw_rk[l], lnw=rw_lnw[l], lnb=rw_lnb[l])
        y_rw_p, s_rw_p = _rwkv(p_rw, jnp.zeros((bp, RW_IN), F32), jnp.zeros((bp, RW_HEADS, RW_HD, RW_HD), F32),
                               rw_prm, batch=bp, seq=tp, row_off=0, nsub=8)
        y_rw_s, s_rw_s = _rwkv(p_rw, state_rwkv_shift[l], state_rwkv[l], rw_prm, batch=bs, seq=ts, row_off=mp, nsub=1)
        past_s = jnp.concatenate([cache_swa_k[l].reshape(bs, WINDOW, kvw), cache_swa_v[l].reshape(bs, WINDOW, kvw)],
                                 axis=-1)
        y_sw_p, kv_p = _swa(p_sw, cos_p, sin_p, jnp.zeros((bp, WINDOW, 2 * kvw), F32), sw_sinks[l],
                            batch=bp, seq=tp, row_off=0, past_valid=False)
        y_sw_s, kv_s = _swa(p_sw, cos_s, sin_s, past_s, sw_sinks[l], batch=bs, seq=ts, row_off=mp, past_valid=True)

        branches = [jnp.concatenate([a, b], axis=0) for a, b in ((y_hg_p, y_hg_s), (y_rw_p, y_rw_s), (y_sw_p, y_sw_s))]
        merged = _gate_merge(h, w_gate_br[l].astype(BF16), branches, w_branch[l].astype(BF16), tm=tm, tn=256)
        x = _matmul(merged, w_out[l].astype(BF16), tm=tm, tn=1024, res=x, name="w_out")

        j = l // 2
        if l % 2 == 0:
            h2 = _rmsnorm(x, norm_ffn[l], BF16, tm_norm)
            act = _glu(h2, ffn_w_gate[j].astype(BF16), ffn_w_up[j].astype(BF16), tm=tm, tn=512)
            x = _matmul(act, ffn_w_down[j].astype(BF16), tm=tm, tn=1024, tk=3584, res=x, name="ffn_down")
        else:
            x = _moe_layer(x, norm_ffn[l], moe_router[j], moe_w_gate[j], moe_w_up[j],
                           moe_w_down[j].astype(BF16), tm=tm_norm, glu_tn=512, down_tn=2048, down_tk=3584)

        outs["hg_p"].append(s_hg_p)
        outs["hg_s"].append(s_hg_s)
        outs["rw_p"].append(s_rw_p)
        outs["rw_s"].append(s_rw_s)
        outs["sh_p"].append(p_rw[tp - 1:mp:tp])
        outs["sh_s"].append(p_rw[mp + ts - 1::ts])
        kv_p = kv_p.reshape(bp, tp, 2, SW_KV_HEADS, SW_HD)[:, -WINDOW:]
        kv_s = kv_s.reshape(bs, ts, 2, SW_KV_HEADS, SW_HD)
        outs["k_p"].append(kv_p[:, :, 0])
        outs["v_p"].append(kv_p[:, :, 1])
        outs["k_s"].append(kv_s[:, :, 0])
        outs["v_s"].append(kv_s[:, :, 1])

    y_p = _rmsnorm(x, norm_final, F32, tm_norm, 0, mp)
    y_s = _rmsnorm(x, norm_final, F32, tm_norm, mp, ms)
    st = {k: jnp.stack(v) for k, v in outs.items()}
    return (y_p.reshape(bp, tp, d), y_s.reshape(bs, ts, d), st["hg_p"], st["hg_s"], st["rw_p"], st["rw_s"],
            st["sh_p"], st["sh_s"], st["k_p"], st["v_p"], st["k_s"], st["v_s"])
```

```python
import functools

import jax
import jax.numpy as jnp
from jax import lax
from jax.experimental import pallas as pl
from jax.experimental.pallas import tpu as pltpu

F32 = jnp.float32
BF16 = jnp.bfloat16

D_MODEL = 4096
CHUNK = 64
BR_WIDTH = 1024
NORM_EPS = 1e-5
MASK_VALUE = -1e30
HG_HEADS = 8
HG_DK = 128
HG_DV = 128
HG_IN = 4 * BR_WIDTH
RW_HEADS = 16
RW_HD = 64
RW_W_RANK = 64
RW_A_RANK = 64
RW_G_RANK = 128
RW_IN = 3 * BR_WIDTH + RW_W_RANK + RW_A_RANK + RW_G_RANK
RW_LN_EPS = 64e-5
SW_HEADS = 16
SW_KV_HEADS = 2
Q_PER_KV = SW_HEADS // SW_KV_HEADS
SW_HD = 64
WINDOW = 128
ROPE_THETA = 10000.0
SW_IN = SW_HEADS * SW_HD + 2 * SW_KV_HEADS * SW_HD
D_FF = 14336
N_EXPERTS = 8
TOP_K = 2
PAST_LEN = 4096

V7X_VMEM_LIMIT_BYTES = 56 * 1024 * 1024
LANES = 128

HG_BLOCK = 16
HG_HEAD_GROUP = 4
RW_CHUNK = 64
RW_HEAD_GROUP = 4
MOE_ROWS = 512
SW_PART_HEADS = 4


def _cparams(semantics):
    return pltpu.CompilerParams(dimension_semantics=semantics, vmem_limit_bytes=V7X_VMEM_LIMIT_BYTES)


def _sigmoid(x):
    return 1.0 / (1.0 + jnp.exp(-x))


def _split2(x):
    hi = x.astype(BF16)
    lo = (x - hi.astype(F32)).astype(BF16)
    return hi, lo


def _dot1(a, b, dims):
    return lax.dot_general(a.astype(BF16), b.astype(BF16), (dims, ((), ())), preferred_element_type=F32)


def _dot3(a, b, dims):
    ah, al = _split2(a)
    bh, bl = _split2(b)
    dg = lambda x, y: lax.dot_general(x, y, (dims, ((), ())), preferred_element_type=F32)
    return dg(ah, bh) + (dg(ah, bl) + dg(al, bh))


_NN = ((1,), (0,))
_NT = ((1,), (1,))
_TN = ((0,), (0,))


def _split3(x):
    hi = x.astype(BF16)
    r1 = x - hi.astype(F32)
    mid = r1.astype(BF16)
    lo = (r1 - mid.astype(F32)).astype(BF16)
    return hi, mid, lo


def _sum_dots(mask, pieces):
    hi, mid, lo = pieces
    d = lambda y: jnp.dot(mask, y, preferred_element_type=F32)
    return d(hi) + (d(mid) + d(lo))


def _cumsum_rows(tri, x):
    return _sum_dots(tri, _split3(x))


def _rmsnorm_kernel(x_ref, g_ref, o_ref):
    x = x_ref[...]
    y = x * lax.rsqrt(jnp.mean(x * x, axis=-1, keepdims=True) + NORM_EPS)
    o_ref[...] = (y * g_ref[...]).astype(o_ref.dtype)


def _rmsnorm(x, g, out_dtype, tm, row_off=0, nrows=None):
    d = x.shape[1]
    m = x.shape[0] if nrows is None else nrows
    rb = row_off // tm
    return pl.pallas_call(
        _rmsnorm_kernel,
        out_shape=jax.ShapeDtypeStruct((m, d), out_dtype),
        grid=(m // tm,),
        in_specs=[pl.BlockSpec((tm, d), lambda i: (rb + i, 0)), pl.BlockSpec((1, d), lambda i: (0, 0))],
        out_specs=pl.BlockSpec((tm, d), lambda i: (i, 0)),
        compiler_params=_cparams(("parallel",)),
        name="rmsnorm",
    )(x, g.reshape(1, d))


def _mm_kernel(x_ref, w_ref, *rest, nk, has_res):
    o_ref = rest[-1]
    part = jnp.dot(x_ref[...], w_ref[...], preferred_element_type=F32)
    if nk == 1:
        if has_res:
            part = part + rest[0][...]
        o_ref[...] = part.astype(o_ref.dtype)
        return
    k = pl.program_id(2)

    @pl.when(k == 0)
    def _():
        o_ref[...] = part + rest[0][...] if has_res else part

    @pl.when(k > 0)
    def _():
        o_ref[...] += part


def _matmul(x, w, *, tm, tn, tk=None, res=None, out_dtype=F32, name="matmul"):
    m, kd = x.shape
    n = w.shape[1]
    tk = kd if tk is None else tk
    nk = kd // tk
    assert m % tm == 0 and n % tn == 0 and kd % tk == 0
    assert nk == 1 or out_dtype == F32
    in_specs = [pl.BlockSpec((tm, tk), lambda i, j, k: (i, k)), pl.BlockSpec((tk, tn), lambda i, j, k: (k, j))]
    args = [x, w]
    if res is not None:
        in_specs.append(pl.BlockSpec((tm, tn), lambda i, j, k: (i, j)))
        args.append(res)
    return pl.pallas_call(
        functools.partial(_mm_kernel, nk=nk, has_res=res is not None),
        out_shape=jax.ShapeDtypeStruct((m, n), out_dtype),
        grid=(m // tm, n // tn, nk),
        in_specs=in_specs,
        out_specs=pl.BlockSpec((tm, tn), lambda i, j, k: (i, j)),
        compiler_params=_cparams(("parallel", "parallel", "arbitrary")),
        name=name,
    )(*args)


def _glu_kernel(x_ref, wg_ref, wu_ref, o_ref):
    x = x_ref[...]
    a = jnp.dot(x, wg_ref[...], preferred_element_type=F32)
    b = jnp.dot(x, wu_ref[...], preferred_element_type=F32)
    o_ref[...] = ((a * _sigmoid(a)) * b).astype(o_ref.dtype)


def _glu(x, wg, wu, *, tm, tn):
    m, kd = x.shape
    n = wg.shape[1]
    return pl.pallas_call(
        _glu_kernel,
        out_shape=jax.ShapeDtypeStruct((m, n), BF16),
        grid=(m // tm, n // tn),
        in_specs=[pl.BlockSpec((tm, kd), lambda i, j: (i, 0)),
                  pl.BlockSpec((kd, tn), lambda i, j: (0, j)),
                  pl.BlockSpec((kd, tn), lambda i, j: (0, j))],
        out_specs=pl.BlockSpec((tm, tn), lambda i, j: (i, j)),
        compiler_params=_cparams(("parallel", "parallel")),
        name="ffn_glu",
    )(x, wg, wu)


def _gate_merge_kernel(h_ref, g0_ref, g1_ref, g2_ref, b0_ref, b1_ref, b2_ref, wb_ref, o_ref):
    h = h_ref[...]
    acc = None
    for n, (g_ref, b_ref) in enumerate(((g0_ref, b0_ref), (g1_ref, b1_ref), (g2_ref, b2_ref))):
        gate = _sigmoid(jnp.dot(h, g_ref[...], preferred_element_type=F32))
        term = gate * jnp.dot(b_ref[...], wb_ref[n], preferred_element_type=F32)
        acc = term if acc is None else acc + term
    o_ref[...] = acc.astype(o_ref.dtype)


def _gate_merge(h, w_gate, branches, w_branch, *, tm, tn):
    m, d = h.shape
    nj = d // tn
    w = branches[0].shape[1]
    gate_spec = lambda n: pl.BlockSpec((d, tn), lambda i, j, n=n: (0, n * nj + j))
    br_spec = pl.BlockSpec((tm, w), lambda i, j: (i, 0))
    return pl.pallas_call(
        _gate_merge_kernel,
        out_shape=jax.ShapeDtypeStruct((m, d), BF16),
        grid=(m // tm, nj),
        in_specs=[pl.BlockSpec((tm, d), lambda i, j: (i, 0)), gate_spec(0), gate_spec(1), gate_spec(2),
                  br_spec, br_spec, br_spec, pl.BlockSpec((3, w, tn), lambda i, j: (0, 0, j))],
        out_specs=pl.BlockSpec((tm, tn), lambda i, j: (i, j)),
        compiler_params=_cparams(("parallel", "parallel")),
        name="gate_merge",
    )(h, w_gate, w_gate, w_gate, branches[0], branches[1], branches[2], w_branch)


def _hgrn_kernel(q_ref, f_ref, i_ref, g_ref, lb_ref, nw_ref, s0_ref, y_ref, so_ref, st_scr, *, blk, nblk, heads):
    tc = pl.program_id(2)

    @pl.when(tc == 0)
    def _():
        for h in range(heads):
            st_scr[h] = s0_ref[0, h].T

    rows = blk * nblk
    row = lax.broadcasted_iota(jnp.int32, (rows, rows), 0)
    col = lax.broadcasted_iota(jnp.int32, (rows, rows), 1)
    in_block = (row - col).astype(jnp.uint32) <= (row & (blk - 1)).astype(jnp.uint32)
    same_block = (row & -blk) == (col & -blk)
    tri = jnp.where(in_block, 1.0, 0.0).astype(BF16)
    ones_bd = jnp.where(same_block, 1.0, 0.0).astype(BF16)
    lb = lb_ref[...]
    q = q_ref[...]
    f = lb + (1.0 - lb) * _sigmoid(f_ref[...])
    log_f = jnp.log(f)
    k = 1.0 - f
    iv = i_ref[...]
    g = g_ref[...]
    pieces = _split3(log_f)
    b = _sum_dots(tri, pieces)
    total = _sum_dots(ones_bd, pieces)
    qe = (q * _sigmoid(q)) * jnp.exp(b)
    ke = k * jnp.exp(-b)
    k_end = k * jnp.exp(total - b)
    decay = jnp.exp(total)
    lanes = [slice(h * HG_DK, (h + 1) * HG_DK) for h in range(heads)]
    blocks = [slice(n * blk, (n + 1) * blk) for n in range(nblk)]
    att = [jnp.where(in_block, _dot1(qe[:, hs], ke[:, hs], _NT), 0.0) for hs in lanes]
    o_intra = [_dot1(a, iv[:, hs], _NN) for a, hs in zip(att, lanes)]
    grams = [[_dot1(iv[sl, hs], k_end[sl, hs], _TN) for sl in blocks] for hs in lanes]
    o_heads = []
    for h, hs in enumerate(lanes):
        states = [st_scr[h]]
        for n, sl in enumerate(blocks):
            states.append(states[n] * decay[sl.start:sl.start + 1, hs] + grams[h][n])
        st_scr[h] = states[nblk]
        o_inter = [_dot1(qe[sl, hs], states[n], _NT) for n, sl in enumerate(blocks)]
        o = o_intra[h] + jnp.concatenate(o_inter, axis=0)
        o_heads.append(o * lax.rsqrt(jnp.mean(o * o, axis=-1, keepdims=True) + NORM_EPS))
    y = jnp.concatenate(o_heads, axis=1) * nw_ref[...]
    y_ref[...] = (y * (g * _sigmoid(g))).astype(y_ref.dtype)

    @pl.when(tc == pl.num_programs(2) - 1)
    def _():
        for h in range(heads):
            so_ref[0, h] = st_scr[h].T


def _hgrn(p_hg, lb, norm_w, s0, *, batch, seq, row_off, tc):
    nt = seq // tc
    rb = row_off // tc
    hg = HG_HEAD_GROUP
    wcols = hg * HG_DV
    sec = BR_WIDTH // wcols
    spec = lambda s: pl.BlockSpec((tc, wcols), lambda b, h, t, s=s: (rb + b * nt + t, s * sec + h))
    vec = pl.BlockSpec((1, wcols), lambda b, h, t: (0, h))
    st_spec = pl.BlockSpec((1, hg, HG_DK, HG_DV), lambda b, h, t: (b, h, 0, 0))
    return pl.pallas_call(
        functools.partial(_hgrn_kernel, blk=HG_BLOCK, nblk=tc // HG_BLOCK, heads=hg),
        out_shape=(jax.ShapeDtypeStruct((batch * seq, BR_WIDTH), BF16),
                   jax.ShapeDtypeStruct((batch, HG_HEADS, HG_DK, HG_DV), F32)),
        grid=(batch, HG_HEADS // hg, nt),
        in_specs=[spec(0), spec(1), spec(2), spec(3), vec, vec, st_spec],
        out_specs=(pl.BlockSpec((tc, wcols), lambda b, h, t: (b * nt + t, h)), st_spec),
        scratch_shapes=[pltpu.VMEM((hg, HG_DV, HG_DK), F32)],
        compiler_params=_cparams(("parallel", "parallel", "arbitrary")),
        name="hgrn2",
    )(p_hg, p_hg, p_hg, p_hg, lb.reshape(1, BR_WIDTH), norm_w.reshape(1, BR_WIDTH), s0)


def _rwkv_kernel(r_ref, k_ref, v_ref, lo_ref, pr_ref, pk_ref, pv_ref, plo_ref, mur_ref, muk_ref, muv_ref, mulo_ref,
                 w0_ref, a0_ref, kk_ref, ka_ref, rk_ref, lnw_ref, lnb_ref, w2_ref, a2_ref, g2_ref, s0_ref,
                 y_ref, so_ref, s_scr, prev_r, prev_k, prev_v, prev_lo, *, chunk, nsub, heads):
    tc = pl.program_id(2)
    hd = RW_HD

    @pl.when(tc == 0)
    def _():
        s_scr[...] = s0_ref[0]
        prev_r[...] = pr_ref[0]
        prev_k[...] = pk_ref[0]
        prev_v[...] = pv_ref[0]
        prev_lo[...] = plo_ref[0]

    rows = chunk * nsub

    def token_shift(x_ref, prev_scr, mu_ref):
        x = x_ref[...]
        rolled = pltpu.roll(x, 1, 0)
        first = lax.broadcasted_iota(jnp.int32, x.shape, 0) == 0
        shifted = jnp.where(first, prev_scr[...], rolled)
        prev_scr[...] = x[rows - 1:rows, :]
        return x + (shifted - x) * mu_ref[...]

    r = token_shift(r_ref, prev_r, mur_ref)
    k = token_shift(k_ref, prev_k, muk_ref)
    v = token_shift(v_ref, prev_v, muv_ref)
    lo = token_shift(lo_ref, prev_lo, mulo_ref)
    wd = lo[:, :RW_W_RANK]
    ad = lo[:, RW_W_RANK:RW_W_RANK + RW_A_RANK]
    gd = lo[:, RW_W_RANK + RW_A_RANK:]

    wl = w0_ref[...] + _dot3(jnp.tanh(wd), w2_ref[...], _NN)
    neg = -wl
    softplus = jnp.maximum(neg, 0.0) + jnp.log(1.0 + jnp.exp(-jnp.abs(neg)))
    log_decay = -jnp.exp(-softplus - 0.5)
    lr = _sigmoid(a0_ref[...] + _dot3(ad, a2_ref[...], _NN))
    gate = _dot3(_sigmoid(gd), g2_ref[...], _NN)
    kk_raw = k * kk_ref[...]
    k_mod = k * (1.0 + (lr - 1.0) * ka_ref[...])
    rk_w = r * k_mod * rk_ref[...]

    row = lax.broadcasted_iota(jnp.int32, (chunk, chunk), 0)
    col = lax.broadcasted_iota(jnp.int32, (chunk, chunk), 1)
    tri = jnp.where(row >= col, 1.0, 0.0).astype(BF16)
    row2 = lax.broadcasted_iota(jnp.int32, (chunk, 2 * chunk), 0)
    col2 = lax.broadcasted_iota(jnp.int32, (chunk, 2 * chunk), 1) & (chunk - 1)
    lower2 = row2 >= col2
    strict2 = row2 > col2

    chains = [(s, j) for s in range(nsub) for j in range(heads)]
    e_last = {}
    ops = {}
    for s in range(nsub):
        rs = slice(s * chunk, (s + 1) * chunk)
        c = _cumsum_rows(tri, log_decay[rs, :])
        c_last = c[chunk - 1:chunk, :]
        e_c = jnp.exp(c)
        e_cprev = jnp.exp(c - log_decay[rs, :])
        e_inv = jnp.exp(-c)
        e_end = jnp.exp(c_last - c)
        e_last[s] = jnp.exp(c_last)
        for j in range(heads):
            cs = slice(j * hd, (j + 1) * hd)
            kkj = kk_raw[rs, cs]
            kkj = kkj / jnp.maximum(jnp.sqrt(jnp.sum(kkj * kkj, axis=-1, keepdims=True)), 1e-12)
            bj = kkj * lr[rs, cs]
            kj = k_mod[rs, cs]
            inv_parts = [bj * e_inv[:, cs], kj * e_inv[:, cs]]
            ops[s, j] = dict(
                a_dec=-kkj * e_cprev[:, cs], r_dec=r[rs, cs] * e_c[:, cs], v=v[rs, cs],
                bk_inv=jnp.concatenate(inv_parts[::-1] if j % 2 else inv_parts, axis=0),
                b_end=bj * e_end[:, cs], k_end=kj * e_end[:, cs])

    b_half = lambda c: slice(chunk, 2 * chunk) if c[1] % 2 else slice(0, chunk)
    k_half = lambda c: slice(0, chunk) if c[1] % 2 else slice(chunk, 2 * chunk)
    pair = {c: _dot1(jnp.concatenate([ops[c]["a_dec"], ops[c]["r_dec"]], axis=0), ops[c]["bk_inv"], _NT)
            for c in chains}
    m_abk = {c: jnp.where(strict2, pair[c][:chunk, :], 0.0) for c in chains}
    n_rbk = {c: jnp.where(lower2, pair[c][chunk:, :], 0.0) for c in chains}
    mv = {c: _dot1(m_abk[c][:, k_half(c)], ops[c]["v"], _NN) for c in chains}

    first_half = col2 == lax.broadcasted_iota(jnp.int32, (chunk, 2 * chunk), 1)
    eye2 = jnp.where(row2 == col2, 1.0, 0.0)
    row_bd = lax.broadcasted_iota(jnp.int32, (2 * chunk, 2 * chunk), 0)
    col_bd = lax.broadcasted_iota(jnp.int32, (2 * chunk, 2 * chunk), 1)
    diag_blocks = (row_bd < chunk) == (col_bd < chunk)
    block_diag = lambda x: jnp.where(diag_blocks, jnp.concatenate([x, x], axis=0), 0.0)
    pairs = [(s, j) for s in range(nsub) for j in range(0, heads, 2)]
    pw2 = {p: jnp.where(first_half, m_abk[p], m_abk[p[0], p[1] + 1]) for p in pairs}
    inv2 = {p: eye2 + pw2[p] for p in pairs}
    for _ in range(chunk.bit_length() - 2):
        pw2 = {p: _dot1(pw2[p], block_diag(pw2[p]), _NN) for p in pairs}
        inv2 = {p: inv2[p] + _dot1(inv2[p], block_diag(pw2[p]), _NN) for p in pairs}
    inv = {}
    for s, j in pairs:
        inv[s, j] = inv2[s, j][:, :chunk]
        inv[s, j + 1] = inv2[s, j][:, chunk:]
    t_amv = {c: _dot1(inv[c], jnp.concatenate([ops[c]["a_dec"], mv[c]], axis=1), _NN) for c in chains}
    nr = {c: _dot1(n_rbk[c][:, b_half(c)], t_amv[c], _NN) for c in chains}
    y_const = {c: nr[c][:, hd:] + _dot1(n_rbk[c][:, k_half(c)], ops[c]["v"], _NN) for c in chains}
    r_eff = {c: ops[c]["r_dec"] + nr[c][:, :hd] for c in chains}
    gram = {c: _dot1(t_amv[c], ops[c]["b_end"], _TN) for c in chains}
    s_const = {c: gram[c][hd:, :] + _dot1(ops[c]["v"], ops[c]["k_end"], _TN) for c in chains}

    y_heads = {}
    for j in range(heads):
        cs = slice(j * hd, (j + 1) * hd)
        st = s_scr[j]
        for s in range(nsub):
            y_heads[s, j] = _dot1(r_eff[s, j], st, _NT) + y_const[s, j]
            st = st * e_last[s][:, cs] + (_dot1(st, gram[s, j][:hd, :], _NN) + s_const[s, j])
        s_scr[j] = st

    for s in range(nsub):
        rs = slice(s * chunk, (s + 1) * chunk)
        outs = []
        for j in range(heads):
            cs = slice(j * hd, (j + 1) * hd)
            y = y_heads[s, j]
            mean = jnp.mean(y, axis=-1, keepdims=True)
            var = jnp.mean(jnp.square(y - mean), axis=-1, keepdims=True)
            yn = (y - mean) * lax.rsqrt(var + RW_LN_EPS) * lnw_ref[:, cs] + lnb_ref[:, cs]
            yn = yn + jnp.sum(rk_w[rs, cs], axis=-1, keepdims=True) * ops[s, j]["v"]
            outs.append(yn * gate[rs, cs])
        y_ref[rs, :] = jnp.concatenate(outs, axis=1).astype(y_ref.dtype)

    @pl.when(tc == pl.num_programs(2) - 1)
    def _():
        so_ref[0] = s_scr[...]


def _rwkv(p_rw, prev, s0, prm, *, batch, seq, row_off, nsub):
    hg = RW_HEAD_GROUP
    wcols = hg * RW_HD
    rows = RW_CHUNK * nsub
    nt = seq // rows
    rb = row_off // rows
    nsec = BR_WIDTH // wcols
    lo_blk = 3 * nsec
    assert RW_IN - 3 * BR_WIDTH == wcols
    tok = lambda s: pl.BlockSpec((rows, wcols), lambda b, g, t, s=s: (rb + b * nt + t, s * nsec + g))
    tok_lo = pl.BlockSpec((rows, wcols), lambda b, g, t: (rb + b * nt + t, lo_blk))
    prv = lambda s: pl.BlockSpec((1, 1, wcols), lambda b, g, t, s=s: (b, 0, s * nsec + g))
    prv_lo = pl.BlockSpec((1, 1, wcols), lambda b, g, t: (b, 0, lo_blk))
    mu = lambda s: pl.BlockSpec((1, wcols), lambda b, g, t, s=s: (0, s * nsec + g))
    mu_lo = pl.BlockSpec((1, wcols), lambda b, g, t: (0, lo_blk))
    vec = pl.BlockSpec((1, wcols), lambda b, g, t: (0, g))
    lora = lambda rank: pl.BlockSpec((rank, wcols), lambda b, g, t: (0, g))
    st_spec = pl.BlockSpec((1, hg, RW_HD, RW_HD), lambda b, g, t: (b, g, 0, 0))
    prev3 = prev.reshape(batch, 1, RW_IN)
    mu2 = prm["mu"].reshape(1, RW_IN)
    v1 = lambda a: a.reshape(1, BR_WIDTH)
    return pl.pallas_call(
        functools.partial(_rwkv_kernel, chunk=RW_CHUNK, nsub=nsub, heads=hg),
        out_shape=(jax.ShapeDtypeStruct((batch * seq, BR_WIDTH), BF16),
                   jax.ShapeDtypeStruct((batch, RW_HEADS, RW_HD, RW_HD), F32)),
        grid=(batch, RW_HEADS // hg, nt),
        in_specs=[tok(0), tok(1), tok(2), tok_lo, prv(0), prv(1), prv(2), prv_lo, mu(0), mu(1), mu(2), mu_lo,
                  vec, vec, vec, vec, vec, vec, vec, lora(RW_W_RANK), lora(RW_A_RANK), lora(RW_G_RANK), st_spec],
        out_specs=(pl.BlockSpec((rows, wcols), lambda b, g, t: (b * nt + t, g)), st_spec),
        scratch_shapes=[pltpu.VMEM((hg, RW_HD, RW_HD), F32)] + [pltpu.VMEM((1, wcols), F32)] * 4,
        compiler_params=_cparams(("parallel", "parallel", "arbitrary")),
        name="rwkv7",
    )(p_rw, p_rw, p_rw, p_rw, prev3, prev3, prev3, prev3, mu2, mu2, mu2, mu2,
      v1(prm["w0"]), v1(prm["a0"]), v1(prm["kk"]), v1(prm["ka"]), v1(prm["rk"]), v1(prm["lnw"]), v1(prm["lnb"]),
      prm["w2"], prm["a2"], prm["g2"], s0)


def _rope_partner(x):
    width = x.shape[-1]
    half = SW_HD // 2
    lane = lax.broadcasted_iota(jnp.int32, x.shape, x.ndim - 1)
    return jnp.where(lane % SW_HD < half, pltpu.roll(x, width - half, x.ndim - 1), pltpu.roll(x, half, x.ndim - 1))


def _swa_kernel(sink_ref, q_ref, kv_ref, cos_ref, sin_ref, past_ref, o_ref, kvo_ref, buf, *, past_valid):
    c = pl.program_id(1)
    kvw = SW_KV_HEADS * SW_HD

    @pl.when(c == 0)
    def _():
        buf[0:WINDOW, :] = past_ref[0]

    @pl.when(c > 0)
    def _():
        buf[0:WINDOW, :] = buf[CHUNK:CHUNK + WINDOW, :]

    cos = cos_ref[...]
    sin = sin_ref[...]
    kv = kv_ref[...]
    k = kv[:, :kvw]
    v = kv[:, kvw:]
    k_rot = k * cos + _rope_partner(k) * sin
    buf[WINDOW:WINDOW + CHUNK, :kvw] = k_rot
    buf[WINDOW:WINDOW + CHUNK, kvw:] = v
    kvo_ref[:, :kvw] = k_rot
    kvo_ref[:, kvw:] = v

    q = q_ref[...]
    reps = q.shape[1] // cos.shape[1]
    cos_q = jnp.concatenate([cos] * reps, axis=1)
    sin_q = jnp.concatenate([sin] * reps, axis=1)
    q_rot = q * cos_q + _rope_partner(q) * sin_q

    band = WINDOW + CHUNK
    part_rows = SW_PART_HEADS * CHUNK
    slot = lax.broadcasted_iota(jnp.int32, (part_rows, band), 1)
    valid = None if past_valid else slot >= (WINDOW // CHUNK - c) * CHUNK
    kgs = [buf[:, g * SW_HD:(g + 1) * SW_HD].astype(BF16) for g in range(SW_KV_HEADS)]
    vgs = [buf[:, kvw + g * SW_HD:kvw + (g + 1) * SW_HD].astype(BF16) for g in range(SW_KV_HEADS)]
    parts = [range(h0, h0 + SW_PART_HEADS) for h0 in range(0, SW_HEADS, SW_PART_HEADS)]
    qs = [jnp.concatenate([q_rot[:, h * SW_HD:(h + 1) * SW_HD] for h in hs], axis=0).astype(BF16) for hs in parts]
    sinks = [jnp.concatenate([jnp.full((CHUNK, 1), sink_ref[h], F32) for h in hs], axis=0) for hs in parts]
    ss = [lax.dot_general(qp, kgs[hs[0] // Q_PER_KV], (_NT, ((), ())), preferred_element_type=F32) * (SW_HD ** -0.5)
          for qp, hs in zip(qs, parts)]
    if valid is not None:
        ss = [jnp.where(valid, s, MASK_VALUE) for s in ss]
    ms = [jnp.maximum(jnp.max(s, axis=-1, keepdims=True), sink) for s, sink in zip(ss, sinks)]
    es = [jnp.exp(s - m) for s, m in zip(ss, ms)]
    dens = [jnp.sum(e, axis=-1, keepdims=True) + jnp.exp(sink - m) for e, sink, m in zip(es, sinks, ms)]
    ogs = [jnp.dot(e.astype(BF16), vgs[hs[0] // Q_PER_KV], preferred_element_type=F32) / den
           for e, hs, den in zip(es, parts, dens)]
    outs = [og[i * CHUNK:(i + 1) * CHUNK, :] for og in ogs for i in range(SW_PART_HEADS)]
    o_ref[...] = jnp.concatenate(outs, axis=1).astype(o_ref.dtype)


def _swa(p_sw, cos_tab, sin_tab, past_kv, sinks, *, batch, seq, row_off, past_valid):
    nc = seq // CHUNK
    rb = row_off // CHUNK
    qw = SW_HEADS * SW_HD
    kvw2 = 2 * SW_KV_HEADS * SW_HD
    grid_spec = pltpu.PrefetchScalarGridSpec(
        num_scalar_prefetch=1,
        grid=(batch, nc),
        in_specs=[pl.BlockSpec((CHUNK, qw), lambda b, c, s: (rb + b * nc + c, 0)),
                  pl.BlockSpec((CHUNK, kvw2), lambda b, c, s: (rb + b * nc + c, qw // kvw2)),
                  pl.BlockSpec((CHUNK, LANES), lambda b, c, s: (c, 0)),
                  pl.BlockSpec((CHUNK, LANES), lambda b, c, s: (c, 0)),
                  pl.BlockSpec((1, WINDOW, kvw2), lambda b, c, s: (b, 0, 0))],
        out_specs=(pl.BlockSpec((CHUNK, qw), lambda b, c, s: (b * nc + c, 0)),
                   pl.BlockSpec((CHUNK, kvw2), lambda b, c, s: (b * nc + c, 0))),
        scratch_shapes=[pltpu.VMEM((WINDOW + CHUNK, kvw2), F32)],
    )
    return pl.pallas_call(
        functools.partial(_swa_kernel, past_valid=past_valid),
        out_shape=(jax.ShapeDtypeStruct((batch * seq, qw), BF16),
                   jax.ShapeDtypeStruct((batch * seq, kvw2), F32)),
        grid_spec=grid_spec,
        compiler_params=_cparams(("parallel", "arbitrary")),
        name="swa",
    )(sinks, p_sw, p_sw, cos_tab, sin_tab, past_kv)


def _rope_tables(pos):
    half = SW_HD // 2
    inv = ROPE_THETA ** (-jnp.arange(half, dtype=F32) / half)
    ang = pos.astype(F32)[:, None] * inv[None, :]
    cos, sin = jnp.cos(ang), jnp.sin(ang)
    reps = LANES // SW_HD
    return jnp.tile(jnp.concatenate([cos, cos], axis=1), (1, reps)), jnp.tile(jnp.concatenate([-sin, sin], axis=1), (1, reps))


def _norm_router_kernel(x_ref, g_ref, wr_ref, h_ref, route_ref):
    x = x_ref[...]
    y = x * lax.rsqrt(jnp.mean(x * x, axis=-1, keepdims=True) + NORM_EPS)
    h = y * g_ref[...]
    h_ref[...] = h
    logits = _dot3(h, wr_ref[...], _NN)
    lane = lax.broadcasted_iota(jnp.int32, logits.shape, 1).astype(F32)
    neg_inf = -jnp.inf
    logits = jnp.where(lane < N_EXPERTS, logits, neg_inf)
    m1 = jnp.max(logits, axis=-1, keepdims=True)
    i1 = jnp.min(jnp.where(logits == m1, lane, float(LANES)), axis=-1, keepdims=True)
    rest = jnp.where(lane == i1, neg_inf, logits)
    m2 = jnp.max(rest, axis=-1, keepdims=True)
    i2 = jnp.min(jnp.where(rest == m2, lane, float(LANES)), axis=-1, keepdims=True)
    e2 = jnp.exp(m2 - m1)
    g1 = 1.0 / (1.0 + e2)
    g2 = e2 / (1.0 + e2)
    route = jnp.where(lane == 0, i1, jnp.where(lane == 1, i2, jnp.where(lane == 2, g1, jnp.where(lane == 3, g2, 0.0))))
    route_ref[...] = route


def _norm_router(x, g, router, *, tm):
    m, d = x.shape
    wr = jnp.zeros((d, LANES), F32).at[:, :N_EXPERTS].set(router)
    return pl.pallas_call(
        _norm_router_kernel,
        out_shape=(jax.ShapeDtypeStruct((m, d), F32), jax.ShapeDtypeStruct((m, LANES), F32)),
        grid=(m // tm,),
        in_specs=[pl.BlockSpec((tm, d), lambda i: (i, 0)), pl.BlockSpec((1, d), lambda i: (0, 0)),
                  pl.BlockSpec((d, LANES), lambda i: (0, 0))],
        out_specs=(pl.BlockSpec((tm, d), lambda i: (i, 0)), pl.BlockSpec((tm, LANES), lambda i: (i, 0))),
        compiler_params=_cparams(("parallel",)),
        name="norm_router",
    )(x, g.reshape(1, d), wr)


def _gather_rows_kernel(idx_ref, nxt_ref, src_hbm, o_ref, buf, sem, *, rows):
    i = pl.program_id(0)
    slot = i % 2

    def row_copy(src_row, s, r):
        return pltpu.make_async_copy(src_hbm.at[pl.ds(src_row, 1)], buf.at[s, pl.ds(r, 1)], sem.at[s])

    def issue(ref, s):
        def start(r, carry):
            row_copy(ref[0, 0, r], s, r).start()
            return carry

        lax.fori_loop(0, rows, start, 0, unroll=8)

    @pl.when(i == 0)
    def _():
        issue(idx_ref, 0)

    @pl.when(i + 1 < pl.num_programs(0))
    def _():
        issue(nxt_ref, 1 - slot)

    def wait(r, carry):
        row_copy(0, slot, r).wait()
        return carry

    lax.fori_loop(0, rows, wait, 0, unroll=8)
    o_ref[...] = buf[slot].astype(o_ref.dtype)


def _gather_rows(src, idx, out_dtype, *, rows):
    n = idx.shape[0]
    nb = n // rows
    d = src.shape[1]
    return pl.pallas_call(
        functools.partial(_gather_rows_kernel, rows=rows),
        out_shape=jax.ShapeDtypeStruct((n, d), out_dtype),
        grid=(nb,),
        in_specs=[pl.BlockSpec((1, 1, rows), lambda i: (i, 0, 0), memory_space=pltpu.SMEM),
                  pl.BlockSpec((1, 1, rows), lambda i: (jnp.minimum(i + 1, nb - 1), 0, 0), memory_space=pltpu.SMEM),
                  pl.BlockSpec(memory_space=pl.ANY)],
        out_specs=pl.BlockSpec((rows, d), lambda i: (i, 0)),
        scratch_shapes=[pltpu.VMEM((2, rows, d), src.dtype), pltpu.SemaphoreType.DMA((2,))],
        compiler_params=_cparams(("arbitrary",)),
        name="moe_gather",
    )(idx.reshape(nb, 1, rows), idx.reshape(nb, 1, rows), src)


def _moe_glu_kernel(be_ref, nv_ref, x_ref, wg_ref, wu_ref, o_ref):
    used = pl.program_id(1) < nv_ref[0]

    @pl.when(used)
    def _():
        x = x_ref[...]
        a = jnp.dot(x, wg_ref[0].astype(BF16), preferred_element_type=F32)
        b = jnp.dot(x, wu_ref[0].astype(BF16), preferred_element_type=F32)
        o_ref[...] = ((a * _sigmoid(a)) * b).astype(o_ref.dtype)

    @pl.when(jnp.logical_not(used))
    def _():
        o_ref[...] = jnp.zeros_like(o_ref)


def _moe_glu(rows_x, wg, wu, block_e, n_valid, *, tn):
    n, d = rows_x.shape
    nb = n // MOE_ROWS
    f = wg.shape[2]
    blk = lambda b, nv: jnp.minimum(b, nv[0] - 1)
    grid_spec = pltpu.PrefetchScalarGridSpec(
        num_scalar_prefetch=2,
        grid=(f // tn, nb),
        in_specs=[pl.BlockSpec((MOE_ROWS, d), lambda j, b, be, nv: (blk(b, nv), 0)),
                  pl.BlockSpec((1, d, tn), lambda j, b, be, nv: (be[blk(b, nv)], 0, j)),
                  pl.BlockSpec((1, d, tn), lambda j, b, be, nv: (be[blk(b, nv)], 0, j))],
        out_specs=pl.BlockSpec((MOE_ROWS, tn), lambda j, b, be, nv: (b, j)),
    )
    return pl.pallas_call(
        _moe_glu_kernel,
        out_shape=jax.ShapeDtypeStruct((n, f), BF16),
        grid_spec=grid_spec,
        compiler_params=_cparams(("arbitrary", "arbitrary")),
        name="moe_glu",
    )(block_e, n_valid, rows_x, wg, wu)


def _moe_down_kernel(be_ref, nv_ref, x_ref, w_ref, o_ref):
    k = pl.program_id(2)

    used = pl.program_id(1) < nv_ref[0]

    @pl.when(used)
    def _():
        part = jnp.dot(x_ref[...], w_ref[0], preferred_element_type=F32)

        @pl.when(k == 0)
        def _():
            o_ref[...] = part

        @pl.when(k > 0)
        def _():
            o_ref[...] += part

    @pl.when(jnp.logical_not(used))
    def _():
        o_ref[...] = jnp.zeros_like(o_ref)


def _moe_down(act, wd, block_e, n_valid, *, tn, tk):
    n, f = act.shape
    nb = n // MOE_ROWS
    d = wd.shape[2]
    blk = lambda b, nv: jnp.minimum(b, nv[0] - 1)
    grid_spec = pltpu.PrefetchScalarGridSpec(
        num_scalar_prefetch=2,
        grid=(d // tn, nb, f // tk),
        in_specs=[pl.BlockSpec((MOE_ROWS, tk), lambda j, b, k, be, nv: (blk(b, nv), k)),
                  pl.BlockSpec((1, tk, tn), lambda j, b, k, be, nv: (be[blk(b, nv)], k, j))],
        out_specs=pl.BlockSpec((MOE_ROWS, tn), lambda j, b, k, be, nv: (b, j)),
    )
    return pl.pallas_call(
        _moe_down_kernel,
        out_shape=jax.ShapeDtypeStruct((n, d), F32),
        grid_spec=grid_spec,
        compiler_params=_cparams(("arbitrary", "arbitrary", "arbitrary")),
        name="moe_down",
    )(block_e, n_valid, act, wd)


def _combine_kernel(p0_ref, p1_ref, n0_ref, n1_ref, x_ref, gate_ref, out_hbm, o_ref, buf, sem, *, rows):
    i = pl.program_id(0)
    slot = i % 2

    def row_copy(src_row, s, k, r):
        return pltpu.make_async_copy(out_hbm.at[pl.ds(src_row, 1)], buf.at[s, k, pl.ds(r, 1)], sem.at[s, k])

    def issue(ref0, ref1, s):
        def start(r, carry):
            row_copy(ref0[0, 0, r], s, 0, r).start()
            row_copy(ref1[0, 0, r], s, 1, r).start()
            return carry

        lax.fori_loop(0, rows, start, 0, unroll=8)

    @pl.when(i == 0)
    def _():
        issue(p0_ref, p1_ref, 0)

    @pl.when(i + 1 < pl.num_programs(0))
    def _():
        issue(n0_ref, n1_ref, 1 - slot)

    def wait(r, carry):
        row_copy(0, slot, 0, r).wait()
        row_copy(0, slot, 1, r).wait()
        return carry

    lax.fori_loop(0, rows, wait, 0, unroll=8)
    gate = gate_ref[...]
    g0 = gate[:, 2:3]
    g1 = gate[:, 3:4]
    o_ref[...] = x_ref[...] + (buf[slot, 0] * g0 + buf[slot, 1] * g1)


def _combine(x, route, out_rows, pos0, pos1, *, rows):
    m, d = x.shape
    nb = m // rows
    idx_spec = pl.BlockSpec((1, 1, rows), lambda i: (i, 0, 0), memory_space=pltpu.SMEM)
    nxt_spec = pl.BlockSpec((1, 1, rows), lambda i: (jnp.minimum(i + 1, nb - 1), 0, 0), memory_space=pltpu.SMEM)
    p0 = pos0.reshape(nb, 1, rows)
    p1 = pos1.reshape(nb, 1, rows)
    return pl.pallas_call(
        functools.partial(_combine_kernel, rows=rows),
        out_shape=jax.ShapeDtypeStruct((m, d), F32),
        grid=(nb,),
        in_specs=[idx_spec, idx_spec, nxt_spec, nxt_spec, pl.BlockSpec((rows, d), lambda i: (i, 0)),
                  pl.BlockSpec((rows, LANES), lambda i: (i, 0)), pl.BlockSpec(memory_space=pl.ANY)],
        out_specs=pl.BlockSpec((rows, d), lambda i: (i, 0)),
        scratch_shapes=[pltpu.VMEM((2, TOP_K, rows, d), F32), pltpu.SemaphoreType.DMA((2, TOP_K))],
        compiler_params=_cparams(("arbitrary",)),
        name="moe_combine",
    )(p0, p1, p0, p1, x, route, out_rows)


def _moe_layer(x, g, router, wg, wu, wd, *, tm, glu_tn, down_tn, down_tk):
    m, d = x.shape
    h, route = _norm_router(x, g, router, tm=tm)
    eid = route[:, :TOP_K].astype(jnp.int32).reshape(-1)
    n_assign = m * TOP_K
    onehot = (eid[:, None] == jnp.arange(N_EXPERTS)[None, :]).astype(jnp.int32)
    rank = jnp.take_along_axis(jnp.cumsum(onehot, axis=0), eid[:, None], axis=1)[:, 0] - 1
    counts = jnp.sum(onehot, axis=0)
    padded = (counts + MOE_ROWS - 1) // MOE_ROWS * MOE_ROWS
    pend = jnp.cumsum(padded)
    pstart = pend - padded
    dest = pstart[eid] + rank
    nb = -(-n_assign // MOE_ROWS) + N_EXPERTS
    tok = jnp.arange(n_assign, dtype=jnp.int32) // TOP_K
    src_tok = jnp.zeros((nb * MOE_ROWS,), jnp.int32).at[dest].set(tok)
    block_e = jnp.minimum(jnp.searchsorted(pend, jnp.arange(nb) * MOE_ROWS, side="right"), N_EXPERTS - 1)
    block_e = block_e.astype(jnp.int32)
    n_valid = (pend[-1] // MOE_ROWS).astype(jnp.int32).reshape(1)
    rows_x = _gather_rows(h, src_tok, BF16, rows=MOE_ROWS)
    act = _moe_glu(rows_x, wg, wu, block_e, n_valid, tn=glu_tn)
    out_rows = _moe_down(act, wd, block_e, n_valid, tn=down_tn, tk=down_tk)
    dest2 = dest.reshape(m, TOP_K).astype(jnp.int32)
    return _combine(x, route, out_rows, dest2[:, 0], dest2[:, 1], rows=256)


def kernel(x_prompt, x_sample, state_hgrn, state_rwkv, state_rwkv_shift, cache_swa_k, cache_swa_v, norm_mix, w_in, hg_lower_bounds, hg_norm, rw_mu, rw_w0, rw_w2, rw_a0, rw_a2, rw_g2, rw_kk, rw_ka, rw_rk, rw_lnw, rw_lnb, sw_sinks, w_branch, w_gate_br, w_out, norm_ffn, ffn_w_gate, ffn_w_up, ffn_w_down, moe_router, moe_w_gate, moe_w_up, moe_w_down, norm_final):
    bp, tp, d = x_prompt.shape
    bs, ts, _ = x_sample.shape
    mp, ms = bp * tp, bs * ts
    m = mp + ms
    depth = w_in.shape[0]
    tm = 768
    tm_norm = 256
    x = jnp.concatenate([x_prompt.reshape(mp, d), x_sample.reshape(ms, d)], axis=0)

    lbs = jax.nn.softmax(hg_lower_bounds.astype(F32), axis=0)
    lbs = jnp.cumsum(lbs, axis=0) - lbs[0]
    cos_p, sin_p = _rope_tables(jnp.arange(tp))
    cos_s, sin_s = _rope_tables(PAST_LEN + jnp.arange(ts))
    kvw = SW_KV_HEADS * SW_HD

    outs = {k: [] for k in ("hg_p", "hg_s", "rw_p", "rw_s", "sh_p", "sh_s", "k_p", "v_p", "k_s", "v_s")}
    for l in range(depth):
        h = _rmsnorm(x, norm_mix[l], BF16, tm_norm)
        w_in_l = w_in[l]
        p_hg = _matmul(h, w_in_l[:, :HG_IN].astype(BF16), tm=tm, tn=1024, name="proj_hg")
        p_rw = _matmul(h, w_in_l[:, HG_IN:HG_IN + RW_IN].astype(BF16), tm=tm, tn=RW_IN // 2, name="proj_rw")
        p_sw = _matmul(h, w_in_l[:, HG_IN + RW_IN:].astype(BF16), tm=tm, tn=SW_IN, name="proj_sw")

        y_hg_p, s_hg_p = _hgrn(p_hg, lbs[l], hg_norm[l], jnp.zeros((bp, HG_HEADS, HG_DK, HG_DV), F32),
                               batch=bp, seq=tp, row_off=0, tc=256)
        y_hg_s, s_hg_s = _hgrn(p_hg, lbs[l], hg_norm[l], state_hgrn[l], batch=bs, seq=ts, row_off=mp, tc=ts)
        rw_prm = dict(mu=rw_mu[l], w0=rw_w0[l], w2=rw_w2[l], a0=rw_a0[l], a2=rw_a2[l], g2=rw_g2[l], kk=rw_kk[l],
                      ka=rw_ka[l], rk=rw_rk[l], lnw=rw_lnw[l], lnb=rw_lnb[l])
        y_rw_p, s_rw_p = _rwkv(p_rw, jnp.zeros((bp, RW_IN), F32), jnp.zeros((bp, RW_HEADS, RW_HD, RW_HD), F32),
                               rw_prm, batch=bp, seq=tp, row_off=0, nsub=8)
        y_rw_s, s_rw_s = _rwkv(p_rw, state_rwkv_shift[l], state_rwkv[l], rw_prm, batch=bs, seq=ts, row_off=mp, nsub=1)
        past_s = jnp.concatenate([cache_swa_k[l].reshape(bs, WINDOW, kvw), cache_swa_v[l].reshape(bs, WINDOW, kvw)],
                                 axis=-1)
        y_sw_p, kv_p = _swa(p_sw, cos_p, sin_p, jnp.zeros((bp, WINDOW, 2 * kvw), F32), sw_sinks[l],
                            batch=bp, seq=tp, row_off=0, past_valid=False)
        y_sw_s, kv_s = _swa(p_sw, cos_s, sin_s, past_s, sw_sinks[l], batch=bs, seq=ts, row_off=mp, past_valid=True)

        branches = [jnp.concatenate([a, b], axis=0) for a, b in ((y_hg_p, y_hg_s), (y_rw_p, y_rw_s), (y_sw_p, y_sw_s))]
        merged = _gate_merge(h, w_gate_br[l].astype(BF16), branches, w_branch[l].astype(BF16), tm=tm, tn=256)
        x = _matmul(merged, w_out[l].astype(BF16), tm=tm, tn=1024, res=x, name="w_out")

        j = l // 2
        if l % 2 == 0:
            h2 = _rmsnorm(x, norm_ffn[l], BF16, tm_norm)
            act = _glu(h2, ffn_w_gate[j].astype(BF16), ffn_w_up[j].astype(BF16), tm=tm, tn=512)
            x = _matmul(act, ffn_w_down[j].astype(BF16), tm=tm, tn=1024, tk=3584, res=x, name="ffn_down")
        else:
            x = _moe_layer(x, norm_ffn[l], moe_router[j], moe_w_gate[j], moe_w_up[j],
                           moe_w_down[j].astype(BF16), tm=tm_norm, glu_tn=512, down_tn=2048, down_tk=3584)

        outs["hg_p"].append(s_hg_p)
        outs["hg_s"].append(s_hg_s)
        outs["rw_p"].append(s_rw_p)
        outs["rw_s"].append(s_rw_s)
        outs["sh_p"].append(p_rw[tp - 1:mp:tp])
        outs["sh_s"].append(p_rw[mp + ts - 1::ts])
        kv_p = kv_p.reshape(bp, tp, 2, SW_KV_HEADS, SW_HD)[:, -WINDOW:]
        kv_s = kv_s.reshape(bs, ts, 2, SW_KV_HEADS, SW_HD)
        outs["k_p"].append(kv_p[:, :, 0])
        outs["v_p"].append(kv_p[:, :, 1])
        outs["k_s"].append(kv_s[:, :, 0])
        outs["v_s"].append(kv_s[:, :, 1])

    y_p = _rmsnorm(x, norm_final, F32, tm_norm, 0, mp)
    y_s = _rmsnorm(x, norm_final, F32, tm_norm, mp, ms)
    st = {k: jnp.stack(v) for k, v in outs.items()}
    return (y_p.reshape(bp, tp, d), y_s.reshape(bs, ts, d), st["hg_p"], st["hg_s"], st["rw_p"], st["rw_s"],
            st["sh_p"], st["sh_s"], st["k_p"], st["v_p"], st["k_s"], st["v_s"])
```

```python
import functools

import jax
import jax.numpy as jnp
from jax import lax
from jax.experimental import pallas as pl
from jax.experimental.pallas import tpu as pltpu

F32 = jnp.float32
BF16 = jnp.bfloat16

D_MODEL = 4096
CHUNK = 64
BR_WIDTH = 1024
NORM_EPS = 1e-5
MASK_VALUE = -1e30
HG_HEADS = 8
HG_DK = 128
HG_DV = 128
HG_IN = 4 * BR_WIDTH
RW_HEADS = 16
RW_HD = 64
RW_W_RANK = 64
RW_A_RANK = 64
RW_G_RANK = 128
RW_IN = 3 * BR_WIDTH + RW_W_RANK + RW_A_RANK + RW_G_RANK
RW_LN_EPS = 64e-5
SW_HEADS = 16
SW_KV_HEADS = 2
Q_PER_KV = SW_HEADS // SW_KV_HEADS
SW_HD = 64
WINDOW = 128
ROPE_THETA = 10000.0
SW_IN = SW_HEADS * SW_HD + 2 * SW_KV_HEADS * SW_HD
D_FF = 14336
N_EXPERTS = 8
TOP_K = 2
PAST_LEN = 4096

V7X_VMEM_LIMIT_BYTES = 56 * 1024 * 1024
LANES = 128

HG_BLOCK = 16
HG_HEAD_GROUP = 4
RW_CHUNK = 64
RW_HEAD_GROUP = 4
MOE_ROWS = 512
SW_PART_HEADS = 4


def _cparams(semantics):
    return pltpu.CompilerParams(dimension_semantics=semantics, vmem_limit_bytes=V7X_VMEM_LIMIT_BYTES)


def _sigmoid(x):
    return 1.0 / (1.0 + jnp.exp(-x))


def _split2(x):
    hi = x.astype(BF16)
    lo = (x - hi.astype(F32)).astype(BF16)
    return hi, lo


def _dot1(a, b, dims):
    return lax.dot_general(a.astype(BF16), b.astype(BF16), (dims, ((), ())), preferred_element_type=F32)


def _dot3(a, b, dims):
    ah, al = _split2(a)
    bh, bl = _split2(b)
    dg = lambda x, y: lax.dot_general(x, y, (dims, ((), ())), preferred_element_type=F32)
    return dg(ah, bh) + (dg(ah, bl) + dg(al, bh))


_NN = ((1,), (0,))
_NT = ((1,), (1,))
_TN = ((0,), (0,))


def _split3(x):
    hi = x.astype(BF16)
    r1 = x - hi.astype(F32)
    mid = r1.astype(BF16)
    lo = (r1 - mid.astype(F32)).astype(BF16)
    return hi, mid, lo


def _sum_dots(mask, pieces):
    hi, mid, lo = pieces
    d = lambda y: jnp.dot(mask, y, preferred_element_type=F32)
    return d(hi) + (d(mid) + d(lo))


def _cumsum_rows(tri, x):
    return _sum_dots(tri, _split3(x))


def _rmsnorm_kernel(x_ref, g_ref, o_ref):
    x = x_ref[...]
    y = x * lax.rsqrt(jnp.mean(x * x, axis=-1, keepdims=True) + NORM_EPS)
    o_ref[...] = (y * g_ref[...]).astype(o_ref.dtype)


def _rmsnorm(x, g, out_dtype, tm, row_off=0, nrows=None):
    d = x.shape[1]
    m = x.shape[0] if nrows is None else nrows
    rb = row_off // tm
    return pl.pallas_call(
        _rmsnorm_kernel,
        out_shape=jax.ShapeDtypeStruct((m, d), out_dtype),
        grid=(m // tm,),
        in_specs=[pl.BlockSpec((tm, d), lambda i: (rb + i, 0)), pl.BlockSpec((1, d), lambda i: (0, 0))],
        out_specs=pl.BlockSpec((tm, d), lambda i: (i, 0)),
        compiler_params=_cparams(("parallel",)),
        name="rmsnorm",
    )(x, g.reshape(1, d))


def _mm_kernel(x_ref, w_ref, *rest, nk, has_res):
    o_ref = rest[-1]
    part = jnp.dot(x_ref[...], w_ref[...], preferred_element_type=F32)
    if nk == 1:
        if has_res:
            part = part + rest[0][...]
        o_ref[...] = part.astype(o_ref.dtype)
        return
    k = pl.program_id(2)

    @pl.when(k == 0)
    def _():
        o_ref[...] = part + rest[0][...] if has_res else part

    @pl.when(k > 0)
    def _():
        o_ref[...] += part


def _matmul(x, w, *, tm, tn, tk=None, res=None, out_dtype=F32, name="matmul"):
    m, kd = x.shape
    n = w.shape[1]
    tk = kd if tk is None else tk
    nk = kd // tk
    assert m % tm == 0 and n % tn == 0 and kd % tk == 0
    assert nk == 1 or out_dtype == F32
    in_specs = [pl.BlockSpec((tm, tk), lambda i, j, k: (i, k)), pl.BlockSpec((tk, tn), lambda i, j, k: (k, j))]
    args = [x, w]
    if res is not None:
        in_specs.append(pl.BlockSpec((tm, tn), lambda i, j, k: (i, j)))
        args.append(res)
    return pl.pallas_call(
        functools.partial(_mm_kernel, nk=nk, has_res=res is not None),
        out_shape=jax.ShapeDtypeStruct((m, n), out_dtype),
        grid=(m // tm, n // tn, nk),
        in_specs=in_specs,
        out_specs=pl.BlockSpec((tm, tn), lambda i, j, k: (i, j)),
        compiler_params=_cparams(("parallel", "parallel", "arbitrary")),
        name=name,
    )(*args)


def _glu_kernel(x_ref, wg_ref, wu_ref, o_ref):
    x = x_ref[...]
    a = jnp.dot(x, wg_ref[...], preferred_element_type=F32)
    b = jnp.dot(x, wu_ref[...], preferred_element_type=F32)
    o_ref[...] = ((a * _sigmoid(a)) * b).astype(o_ref.dtype)


def _glu(x, wg, wu, *, tm, tn):
    m, kd = x.shape
    n = wg.shape[1]
    return pl.pallas_call(
        _glu_kernel,
        out_shape=jax.ShapeDtypeStruct((m, n), BF16),
        grid=(m // tm, n // tn),
        in_specs=[pl.BlockSpec((tm, kd), lambda i, j: (i, 0)),
                  pl.BlockSpec((kd, tn), lambda i, j: (0, j)),
                  pl.BlockSpec((kd, tn), lambda i, j: (0, j))],
        out_specs=pl.BlockSpec((tm, tn), lambda i, j: (i, j)),
        compiler_params=_cparams(("parallel", "parallel")),
        name="ffn_glu",
    )(x, wg, wu)


def _gate_merge_kernel(h_ref, g0_ref, g1_ref, g2_ref, b0_ref, b1_ref, b2_ref, wb_ref, o_ref):
    h = h_ref[...]
    acc = None
    for n, (g_ref, b_ref) in enumerate(((g0_ref, b0_ref), (g1_ref, b1_ref), (g2_ref, b2_ref))):
        gate = _sigmoid(jnp.dot(h, g_ref[...], preferred_element_type=F32))
        term = gate * jnp.dot(b_ref[...], wb_ref[n], preferred_element_type=F32)
        acc = term if acc is None else acc + term
    o_ref[...] = acc.astype(o_ref.dtype)


def _gate_merge(h, w_gate, branches, w_branch, *, tm, tn):
    m, d = h.shape
    nj = d // tn
    w = branches[0].shape[1]
    gate_spec = lambda n: pl.BlockSpec((d, tn), lambda i, j, n=n: (0, n * nj + j))
    br_spec = pl.BlockSpec((tm, w), lambda i, j: (i, 0))
    return pl.pallas_call(
        _gate_merge_kernel,
        out_shape=jax.ShapeDtypeStruct((m, d), BF16),
        grid=(m // tm, nj),
        in_specs=[pl.BlockSpec((tm, d), lambda i, j: (i, 0)), gate_spec(0), gate_spec(1), gate_spec(2),
                  br_spec, br_spec, br_spec, pl.BlockSpec((3, w, tn), lambda i, j: (0, 0, j))],
        out_specs=pl.BlockSpec((tm, tn), lambda i, j: (i, j)),
        compiler_params=_cparams(("parallel", "parallel")),
        name="gate_merge",
    )(h, w_gate, w_gate, w_gate, branches[0], branches[1], branches[2], w_branch)


def _hgrn_kernel(q_ref, f_ref, i_ref, g_ref, lb_ref, nw_ref, s0_ref, y_ref, so_ref, st_scr, *, blk, nblk, heads):
    tc = pl.program_id(2)

    @pl.when(tc == 0)
    def _():
        for h in range(heads):
            st_scr[h] = s0_ref[0, h].T

    rows = blk * nblk
    row = lax.broadcasted_iota(jnp.int32, (rows, rows), 0)
    col = lax.broadcasted_iota(jnp.int32, (rows, rows), 1)
    in_block = (row - col).astype(jnp.uint32) <= (row & (blk - 1)).astype(jnp.uint32)
    same_block = (row & -blk) == (col & -blk)
    tri = jnp.where(in_block, 1.0, 0.0).astype(BF16)
    ones_bd = jnp.where(same_block, 1.0, 0.0).astype(BF16)
    lb = lb_ref[...]
    q = q_ref[...]
    f = lb + (1.0 - lb) * _sigmoid(f_ref[...])
    log_f = jnp.log(f)
    k = 1.0 - f
    iv = i_ref[...]
    g = g_ref[...]
    pieces = _split3(log_f)
    b = _sum_dots(tri, pieces)
    total = _sum_dots(ones_bd, pieces)
    qe = (q * _sigmoid(q)) * jnp.exp(b)
    ke = k * jnp.exp(-b)
    k_end = k * jnp.exp(total - b)
    decay = jnp.exp(total)
    lanes = [slice(h * HG_DK, (h + 1) * HG_DK) for h in range(heads)]
    blocks = [slice(n * blk, (n + 1) * blk) for n in range(nblk)]
    att = [jnp.where(in_block, _dot1(qe[:, hs], ke[:, hs], _NT), 0.0) for hs in lanes]
    o_intra = [_dot1(a, iv[:, hs], _NN) for a, hs in zip(att, lanes)]
    grams = [[_dot1(iv[sl, hs], k_end[sl, hs], _TN) for sl in blocks] for hs in lanes]
    o_heads = []
    for h, hs in enumerate(lanes):
        states = [st_scr[h]]
        for n, sl in enumerate(blocks):
            states.append(states[n] * decay[sl.start:sl.start + 1, hs] + grams[h][n])
        st_scr[h] = states[nblk]
        o_inter = [_dot1(qe[sl, hs], states[n], _NT) for n, sl in enumerate(blocks)]
        o = o_intra[h] + jnp.concatenate(o_inter, axis=0)
        o_heads.append(o * lax.rsqrt(jnp.mean(o * o, axis=-1, keepdims=True) + NORM_EPS))
    y = jnp.concatenate(o_heads, axis=1) * nw_ref[...]
    y_ref[...] = (y * (g * _sigmoid(g))).astype(y_ref.dtype)

    @pl.when(tc == pl.num_programs(2) - 1)
    def _():
        for h in range(heads):
            so_ref[0, h] = st_scr[h].T


def _hgrn(p_hg, lb, norm_w, s0, *, batch, seq, row_off, tc):
    nt = seq // tc
    rb = row_off // tc
    hg = HG_HEAD_GROUP
    wcols = hg * HG_DV
    sec = BR_WIDTH // wcols
    spec = lambda s: pl.BlockSpec((tc, wcols), lambda b, h, t, s=s: (rb + b * nt + t, s * sec + h))
    vec = pl.BlockSpec((1, wcols), lambda b, h, t: (0, h))
    st_spec = pl.BlockSpec((1, hg, HG_DK, HG_DV), lambda b, h, t: (b, h, 0, 0))
    return pl.pallas_call(
        functools.partial(_hgrn_kernel, blk=HG_BLOCK, nblk=tc // HG_BLOCK, heads=hg),
        out_shape=(jax.ShapeDtypeStruct((batch * seq, BR_WIDTH), BF16),
                   jax.ShapeDtypeStruct((batch, HG_HEADS, HG_DK, HG_DV), F32)),
        grid=(batch, HG_HEADS // hg, nt),
        in_specs=[spec(0), spec(1), spec(2), spec(3), vec, vec, st_spec],
        out_specs=(pl.BlockSpec((tc, wcols), lambda b, h, t: (b * nt + t, h)), st_spec),
        scratch_shapes=[pltpu.VMEM((hg, HG_DV, HG_DK), F32)],
        compiler_params=_cparams(("parallel", "parallel", "arbitrary")),
        name="hgrn2",
    )(p_hg, p_hg, p_hg, p_hg, lb.reshape(1, BR_WIDTH), norm_w.reshape(1, BR_WIDTH), s0)


def _rwkv_kernel(r_ref, k_ref, v_ref, lo_ref, pr_ref, pk_ref, pv_ref, plo_ref, mur_ref, muk_ref, muv_ref, mulo_ref,
                 w0_ref, a0_ref, kk_ref, ka_ref, rk_ref, lnw_ref, lnb_ref, w2_ref, a2_ref, g2_ref, s0_ref,
                 y_ref, so_ref, s_scr, prev_r, prev_k, prev_v, prev_lo, *, chunk, nsub, heads):
    tc = pl.program_id(2)
    hd = RW_HD

    @pl.when(tc == 0)
    def _():
        s_scr[...] = s0_ref[0]
        prev_r[...] = pr_ref[0]
        prev_k[...] = pk_ref[0]
        prev_v[...] = pv_ref[0]
        prev_lo[...] = plo_ref[0]

    rows = chunk * nsub

    def token_shift(x_ref, prev_scr, mu_ref):
        x = x_ref[...]
        rolled = pltpu.roll(x, 1, 0)
        first = lax.broadcasted_iota(jnp.int32, x.shape, 0) == 0
        shifted = jnp.where(first, prev_scr[...], rolled)
        prev_scr[...] = x[rows - 1:rows, :]
        return x + (shifted - x) * mu_ref[...]

    r = token_shift(r_ref, prev_r, mur_ref)
    k = token_shift(k_ref, prev_k, muk_ref)
    v = token_shift(v_ref, prev_v, muv_ref)
    lo = token_shift(lo_ref, prev_lo, mulo_ref)
    wd = lo[:, :RW_W_RANK]
    ad = lo[:, RW_W_RANK:RW_W_RANK + RW_A_RANK]
    gd = lo[:, RW_W_RANK + RW_A_RANK:]

    wl = w0_ref[...] + _dot3(jnp.tanh(wd), w2_ref[...], _NN)
    neg = -wl
    softplus = jnp.maximum(neg, 0.0) + jnp.log(1.0 + jnp.exp(-jnp.abs(neg)))
    log_decay = -jnp.exp(-softplus - 0.5)
    lr = _sigmoid(a0_ref[...] + _dot3(ad, a2_ref[...], _NN))
    gate = _dot3(_sigmoid(gd), g2_ref[...], _NN)
    kk_raw = k * kk_ref[...]
    k_mod = k * (1.0 + (lr - 1.0) * ka_ref[...])
    rk_w = r * k_mod * rk_ref[...]

    row = lax.broadcasted_iota(jnp.int32, (chunk, chunk), 0)
    col = lax.broadcasted_iota(jnp.int32, (chunk, chunk), 1)
    tri = jnp.where(row >= col, 1.0, 0.0).astype(BF16)
    row2 = lax.broadcasted_iota(jnp.int32, (chunk, 2 * chunk), 0)
    col2 = lax.broadcasted_iota(jnp.int32, (chunk, 2 * chunk), 1) & (chunk - 1)
    lower2 = row2 >= col2
    strict2 = row2 > col2

    chains = [(s, j) for s in range(nsub) for j in range(heads)]
    e_last = {}
    ops = {}
    for s in range(nsub):
        rs = slice(s * chunk, (s + 1) * chunk)
        c = _cumsum_rows(tri, log_decay[rs, :])
        c_last = c[chunk - 1:chunk, :]
        e_c = jnp.exp(c)
        e_cprev = jnp.exp(c - log_decay[rs, :])
        e_inv = jnp.exp(-c)
        e_end = jnp.exp(c_last - c)
        e_last[s] = jnp.exp(c_last)
        for j in range(heads):
            cs = slice(j * hd, (j + 1) * hd)
            kkj = kk_raw[rs, cs]
            kkj = kkj / jnp.maximum(jnp.sqrt(jnp.sum(kkj * kkj, axis=-1, keepdims=True)), 1e-12)
            bj = kkj * lr[rs, cs]
            kj = k_mod[rs, cs]
            inv_parts = [bj * e_inv[:, cs], kj * e_inv[:, cs]]
            ops[s, j] = dict(
                a_dec=-kkj * e_cprev[:, cs], r_dec=r[rs, cs] * e_c[:, cs], v=v[rs, cs],
                bk_inv=jnp.concatenate(inv_parts[::-1] if j % 2 else inv_parts, axis=0),
                b_end=bj * e_end[:, cs], k_end=kj * e_end[:, cs])

    b_half = lambda c: slice(chunk, 2 * chunk) if c[1] % 2 else slice(0, chunk)
    k_half = lambda c: slice(0, chunk) if c[1] % 2 else slice(chunk, 2 * chunk)
    pair = {c: _dot1(jnp.concatenate([ops[c]["a_dec"], ops[c]["r_dec"]], axis=0), ops[c]["bk_inv"], _NT)
            for c in chains}
    m_abk = {c: jnp.where(strict2, pair[c][:chunk, :], 0.0) for c in chains}
    n_rbk = {c: jnp.where(lower2, pair[c][chunk:, :], 0.0) for c in chains}
    mv = {c: _dot1(m_abk[c][:, k_half(c)], ops[c]["v"], _NN) for c in chains}

    first_half = col2 == lax.broadcasted_iota(jnp.int32, (chunk, 2 * chunk), 1)
    eye2 = jnp.where(row2 == col2, 1.0, 0.0)
    row_bd = lax.broadcasted_iota(jnp.int32, (2 * chunk, 2 * chunk), 0)
    col_bd = lax.broadcasted_iota(jnp.int32, (2 * chunk, 2 * chunk), 1)
    diag_blocks = (row_bd < chunk) == (col_bd < chunk)
    block_diag = lambda x: jnp.where(diag_blocks, jnp.concatenate([x, x], axis=0), 0.0)
    pairs = [(s, j) for s in range(nsub) for j in range(0, heads, 2)]
    pw2 = {p: jnp.where(first_half, m_abk[p], m_abk[p[0], p[1] + 1]) for p in pairs}
    inv2 = {p: eye2 + pw2[p] for p in pairs}
    for _ in range(chunk.bit_length() - 2):
        pw2 = {p: _dot1(pw2[p], block_diag(pw2[p]), _NN) for p in pairs}
        inv2 = {p: inv2[p] + _dot1(inv2[p], block_diag(pw2[p]), _NN) for p in pairs}
    inv = {}
    for s, j in pairs:
        inv[s, j] = inv2[s, j][:, :chunk]
        inv[s, j + 1] = inv2[s, j][:, chunk:]
    t_amv = {c: _dot1(inv[c], jnp.concatenate([ops[c]["a_dec"], mv[c]], axis=1), _NN) for c in chains}
    nr = {c: _dot1(n_rbk[c][:, b_half(c)], t_amv[c], _NN) for c in chains}
    y_const = {c: nr[c][:, hd:] + _dot1(n_rbk[c][:, k_half(c)], ops[c]["v"], _NN) for c in chains}
    r_eff = {c: ops[c]["r_dec"] + nr[c][:, :hd] for c in chains}
    gram = {c: _dot1(t_amv[c], ops[c]["b_end"], _TN) for c in chains}
    s_const = {c: gram[c][hd:, :] + _dot1(ops[c]["v"], ops[c]["k_end"], _TN) for c in chains}

    y_heads = {}
    for j in range(heads):
        cs = slice(j * hd, (j + 1) * hd)
        st = s_scr[j]
        for s in range(nsub):
            y_heads[s, j] = _dot1(r_eff[s, j], st, _NT) + y_const[s, j]
            st = st * e_last[s][:, cs] + (_dot1(st, gram[s, j][:hd, :], _NN) + s_const[s, j])
        s_scr[j] = st

    for s in range(nsub):
        rs = slice(s * chunk, (s + 1) * chunk)
        outs = []
        for j in range(heads):
            cs = slice(j * hd, (j + 1) * hd)
            y = y_heads[s, j]
            mean = jnp.mean(y, axis=-1, keepdims=True)
            var = jnp.mean(jnp.square(y - mean), axis=-1, keepdims=True)
            yn = (y - mean) * lax.rsqrt(var + RW_LN_EPS) * lnw_ref[:, cs] + lnb_ref[:, cs]
            yn = yn + jnp.sum(rk_w[rs, cs], axis=-1, keepdims=True) * ops[s, j]["v"]
            outs.append(yn * gate[rs, cs])
        y_ref[rs, :] = jnp.concatenate(outs, axis=1).astype(y_ref.dtype)

    @pl.when(tc == pl.num_programs(2) - 1)
    def _():
        so_ref[0] = s_scr[...]


def _rwkv(p_rw, prev, s0, prm, *, batch, seq, row_off, nsub):
    hg = RW_HEAD_GROUP
    wcols = hg * RW_HD
    rows = RW_CHUNK * nsub
    nt = seq // rows
    rb = row_off // rows
    nsec = BR_WIDTH // wcols
    lo_blk = 3 * nsec
    assert RW_IN - 3 * BR_WIDTH == wcols
    tok = lambda s: pl.BlockSpec((rows, wcols), lambda b, g, t, s=s: (rb + b * nt + t, s * nsec + g))
    tok_lo = pl.BlockSpec((rows, wcols), lambda b, g, t: (rb + b * nt + t, lo_blk))
    prv = lambda s: pl.BlockSpec((1, 1, wcols), lambda b, g, t, s=s: (b, 0, s * nsec + g))
    prv_lo = pl.BlockSpec((1, 1, wcols), lambda b, g, t: (b, 0, lo_blk))
    mu = lambda s: pl.BlockSpec((1, wcols), lambda b, g, t, s=s: (0, s * nsec + g))
    mu_lo = pl.BlockSpec((1, wcols), lambda b, g, t: (0, lo_blk))
    vec = pl.BlockSpec((1, wcols), lambda b, g, t: (0, g))
    lora = lambda rank: pl.BlockSpec((rank, wcols), lambda b, g, t: (0, g))
    st_spec = pl.BlockSpec((1, hg, RW_HD, RW_HD), lambda b, g, t: (b, g, 0, 0))
    prev3 = prev.reshape(batch, 1, RW_IN)
    mu2 = prm["mu"].reshape(1, RW_IN)
    v1 = lambda a: a.reshape(1, BR_WIDTH)
    return pl.pallas_call(
        functools.partial(_rwkv_kernel, chunk=RW_CHUNK, nsub=nsub, heads=hg),
        out_shape=(jax.ShapeDtypeStruct((batch * seq, BR_WIDTH), BF16),
                   jax.ShapeDtypeStruct((batch, RW_HEADS, RW_HD, RW_HD), F32)),
        grid=(batch, RW_HEADS // hg, nt),
        in_specs=[tok(0), tok(1), tok(2), tok_lo, prv(0), prv(1), prv(2), prv_lo, mu(0), mu(1), mu(2), mu_lo,
                  vec, vec, vec, vec, vec, vec, vec, lora(RW_W_RANK), lora(RW_A_RANK), lora(RW_G_RANK), st_spec],
        out_specs=(pl.BlockSpec((rows, wcols), lambda b, g, t: (b * nt + t, g)), st_spec),
        scratch_shapes=[pltpu.VMEM((hg, RW_HD, RW_HD), F32)] + [pltpu.VMEM((1, wcols), F32)] * 4,
        compiler_params=_cparams(("parallel", "parallel", "arbitrary")),
        name="rwkv7",
    )(p_rw, p_rw, p_rw, p_rw, prev3, prev3, prev3, prev3, mu2, mu2, mu2, mu2,
      v1(prm["w0"]), v1(prm["a0"]), v1(prm["kk"]), v1(prm["ka"]), v1(prm["rk"]), v1(prm["lnw"]), v1(prm["lnb"]),
      prm["w2"], prm["a2"], prm["g2"], s0)


def _rope_partner(x):
    width = x.shape[-1]
    half = SW_HD // 2
    lane = lax.broadcasted_iota(jnp.int32, x.shape, x.ndim - 1)
    return jnp.where(lane % SW_HD < half, pltpu.roll(x, width - half, x.ndim - 1), pltpu.roll(x, half, x.ndim - 1))


def _swa_kernel(sink_ref, q_ref, kv_ref, cos_ref, sin_ref, past_ref, o_ref, kvo_ref, buf, *, past_valid):
    c = pl.program_id(1)
    kvw = SW_KV_HEADS * SW_HD

    @pl.when(c == 0)
    def _():
        buf[0:WINDOW, :] = past_ref[0]

    @pl.when(c > 0)
    def _():
        buf[0:WINDOW, :] = buf[CHUNK:CHUNK + WINDOW, :]

    cos = cos_ref[...]
    sin = sin_ref[...]
    kv = kv_ref[...]
    k = kv[:, :kvw]
    v = kv[:, kvw:]
    k_rot = k * cos + _rope_partner(k) * sin
    buf[WINDOW:WINDOW + CHUNK, :kvw] = k_rot
    buf[WINDOW:WINDOW + CHUNK, kvw:] = v
    kvo_ref[:, :kvw] = k_rot
    kvo_ref[:, kvw:] = v

    q = q_ref[...]
    reps = q.shape[1] // cos.shape[1]
    cos_q = jnp.concatenate([cos] * reps, axis=1)
    sin_q = jnp.concatenate([sin] * reps, axis=1)
    q_rot = q * cos_q + _rope_partner(q) * sin_q

    band = WINDOW + CHUNK
    part_rows = SW_PART_HEADS * CHUNK
    slot = lax.broadcasted_iota(jnp.int32, (part_rows, band), 1)
    valid = None if past_valid else slot >= (WINDOW // CHUNK - c) * CHUNK
    kgs = [buf[:, g * SW_HD:(g + 1) * SW_HD].astype(BF16) for g in range(SW_KV_HEADS)]
    vgs = [buf[:, kvw + g * SW_HD:kvw + (g + 1) * SW_HD].astype(BF16) for g in range(SW_KV_HEADS)]
    parts = [range(h0, h0 + SW_PART_HEADS) for h0 in range(0, SW_HEADS, SW_PART_HEADS)]
    qs = [jnp.concatenate([q_rot[:, h * SW_HD:(h + 1) * SW_HD] for h in hs], axis=0).astype(BF16) for hs in parts]
    sinks = [jnp.concatenate([jnp.full((CHUNK, 1), sink_ref[h], F32) for h in hs], axis=0) for hs in parts]
    ss = [lax.dot_general(qp, kgs[hs[0] // Q_PER_KV], (_NT, ((), ())), preferred_element_type=F32) * (SW_HD ** -0.5)
          for qp, hs in zip(qs, parts)]
    if valid is not None:
        ss = [jnp.where(valid, s, MASK_VALUE) for s in ss]
    ms = [jnp.maximum(jnp.max(s, axis=-1, keepdims=True), sink) for s, sink in zip(ss, sinks)]
    es = [jnp.exp(s - m) for s, m in zip(ss, ms)]
    dens = [jnp.sum(e, axis=-1, keepdims=True) + jnp.exp(sink - m) for e, sink, m in zip(es, sinks, ms)]
    ogs = [jnp.dot(e.astype(BF16), vgs[hs[0] // Q_PER_KV], preferred_element_type=F32) / den
           for e, hs, den in zip(es, parts, dens)]
    outs = [og[i * CHUNK:(i + 1) * CHUNK, :] for og in ogs for i in range(SW_PART_HEADS)]
    o_ref[...] = jnp.concatenate(outs, axis=1).astype(o_ref.dtype)


def _swa(p_sw, cos_tab, sin_tab, past_kv, sinks, *, batch, seq, row_off, past_valid):
    nc = seq // CHUNK
    rb = row_off // CHUNK
    qw = SW_HEADS * SW_HD
    kvw2 = 2 * SW_KV_HEADS * SW_HD
    grid_spec = pltpu.PrefetchScalarGridSpec(
        num_scalar_prefetch=1,
        grid=(batch, nc),
        in_specs=[pl.BlockSpec((CHUNK, qw), lambda b, c, s: (rb + b * nc + c, 0)),
                  pl.BlockSpec((CHUNK, kvw2), lambda b, c, s: (rb + b * nc + c, qw // kvw2)),
                  pl.BlockSpec((CHUNK, LANES), lambda b, c, s: (c, 0)),
                  pl.BlockSpec((CHUNK, LANES), lambda b, c, s: (c, 0)),
                  pl.BlockSpec((1, WINDOW, kvw2), lambda b, c, s: (b, 0, 0))],
        out_specs=(pl.BlockSpec((CHUNK, qw), lambda b, c, s: (b * nc + c, 0)),
                   pl.BlockSpec((CHUNK, kvw2), lambda b, c, s: (b * nc + c, 0))),
        scratch_shapes=[pltpu.VMEM((WINDOW + CHUNK, kvw2), F32)],
    )
    return pl.pallas_call(
        functools.partial(_swa_kernel, past_valid=past_valid),
        out_shape=(jax.ShapeDtypeStruct((batch * seq, qw), BF16),
                   jax.ShapeDtypeStruct((batch * seq, kvw2), F32)),
        grid_spec=grid_spec,
        compiler_params=_cparams(("parallel", "arbitrary")),
        name="swa",
    )(sinks, p_sw, p_sw, cos_tab, sin_tab, past_kv)


def _rope_tables(pos):
    half = SW_HD // 2
    inv = ROPE_THETA ** (-jnp.arange(half, dtype=F32) / half)
    ang = pos.astype(F32)[:, None] * inv[None, :]
    cos, sin = jnp.cos(ang), jnp.sin(ang)
    reps = LANES // SW_HD
    return jnp.tile(jnp.concatenate([cos, cos], axis=1), (1, reps)), jnp.tile(jnp.concatenate([-sin, sin], axis=1), (1, reps))


def _norm_router_kernel(x_ref, g_ref, wr_ref, h_ref, route_ref):
    x = x_ref[...]
    y = x * lax.rsqrt(jnp.mean(x * x, axis=-1, keepdims=True) + NORM_EPS)
    h = y * g_ref[...]
    h_ref[...] = h
    logits = _dot3(h, wr_ref[...], _NN)
    lane = lax.broadcasted_iota(jnp.int32, logits.shape, 1).astype(F32)
    neg_inf = -jnp.inf
    logits = jnp.where(lane < N_EXPERTS, logits, neg_inf)
    m1 = jnp.max(logits, axis=-1, keepdims=True)
    i1 = jnp.min(jnp.where(logits == m1, lane, float(LANES)), axis=-1, keepdims=True)
    rest = jnp.where(lane == i1, neg_inf, logits)
    m2 = jnp.max(rest, axis=-1, keepdims=True)
    i2 = jnp.min(jnp.where(rest == m2, lane, float(LANES)), axis=-1, keepdims=True)
    e2 = jnp.exp(m2 - m1)
    g1 = 1.0 / (1.0 + e2)
    g2 = e2 / (1.0 + e2)
    route = jnp.where(lane == 0, i1, jnp.where(lane == 1, i2, jnp.where(lane == 2, g1, jnp.where(lane == 3, g2, 0.0))))
    route_ref[...] = route


def _norm_router(x, g, router, *, tm):
    m, d = x.shape
    wr = jnp.zeros((d, LANES), F32).at[:, :N_EXPERTS].set(router)
    return pl.pallas_call(
        _norm_router_kernel,
        out_shape=(jax.ShapeDtypeStruct((m, d), F32), jax.ShapeDtypeStruct((m, LANES), F32)),
        grid=(m // tm,),
        in_specs=[pl.BlockSpec((tm, d), lambda i: (i, 0)), pl.BlockSpec((1, d), lambda i: (0, 0)),
                  pl.BlockSpec((d, LANES), lambda i: (0, 0))],
        out_specs=(pl.BlockSpec((tm, d), lambda i: (i, 0)), pl.BlockSpec((tm, LANES), lambda i: (i, 0))),
        compiler_params=_cparams(("parallel",)),
        name="norm_router",
    )(x, g.reshape(1, d), wr)


def _gather_rows_kernel(idx_ref, nxt_ref, src_hbm, o_ref, buf, sem, *, rows):
    i = pl.program_id(0)
    slot = i % 2

    def row_copy(src_row, s, r):
        return pltpu.make_async_copy(src_hbm.at[pl.ds(src_row, 1)], buf.at[s, pl.ds(r, 1)], sem.at[s])

    def issue(ref, s):
        def start(r, carry):
            row_copy(ref[0, 0, r], s, r).start()
            return carry

        lax.fori_loop(0, rows, start, 0, unroll=8)

    @pl.when(i == 0)
    def _():
        issue(idx_ref, 0)

    @pl.when(i + 1 < pl.num_programs(0))
    def _():
        issue(nxt_ref, 1 - slot)

    def wait(r, carry):
        row_copy(0, slot, r).wait()
        return carry

    lax.fori_loop(0, rows, wait, 0, unroll=8)
    o_ref[...] = buf[slot].astype(o_ref.dtype)


def _gather_rows(src, idx, out_dtype, *, rows):
    n = idx.shape[0]
    nb = n // rows
    d = src.shape[1]
    return pl.pallas_call(
        functools.partial(_gather_rows_kernel, rows=rows),
        out_shape=jax.ShapeDtypeStruct((n, d), out_dtype),
        grid=(nb,),
        in_specs=[pl.BlockSpec((1, 1, rows), lambda i: (i, 0, 0), memory_space=pltpu.SMEM),
                  pl.BlockSpec((1, 1, rows), lambda i: (jnp.minimum(i + 1, nb - 1), 0, 0), memory_space=pltpu.SMEM),
                  pl.BlockSpec(memory_space=pl.ANY)],
        out_specs=pl.BlockSpec((rows, d), lambda i: (i, 0)),
        scratch_shapes=[pltpu.VMEM((2, rows, d), src.dtype), pltpu.SemaphoreType.DMA((2,))],
        compiler_params=_cparams(("arbitrary",)),
        name="moe_gather",
    )(idx.reshape(nb, 1, rows), idx.reshape(nb, 1, rows), src)


def _moe_glu_kernel(be_ref, nv_ref, x_ref, wg_ref, wu_ref, o_ref, wg_bf, wu_bf):
    blk = pl.program_id(1)
    last = nv_ref[0] - 1
    used = blk <= last
    new_tile = (blk == 0) | (be_ref[jnp.minimum(blk, last)] != be_ref[jnp.minimum(jnp.maximum(blk - 1, 0), last)])

    @pl.when(new_tile)
    def _():
        wg_bf[...] = wg_ref[0].astype(BF16)
        wu_bf[...] = wu_ref[0].astype(BF16)

    @pl.when(used)
    def _():
        x = x_ref[...]
        a = jnp.dot(x, wg_bf[...], preferred_element_type=F32)
        b = jnp.dot(x, wu_bf[...], preferred_element_type=F32)
        o_ref[...] = ((a * _sigmoid(a)) * b).astype(o_ref.dtype)

    @pl.when(jnp.logical_not(used))
    def _():
        o_ref[...] = jnp.zeros_like(o_ref)


def _moe_glu(rows_x, wg, wu, block_e, n_valid, *, tn):
    n, d = rows_x.shape
    nb = n // MOE_ROWS
    f = wg.shape[2]
    blk = lambda b, nv: jnp.minimum(b, nv[0] - 1)
    grid_spec = pltpu.PrefetchScalarGridSpec(
        num_scalar_prefetch=2,
        grid=(f // tn, nb),
        in_specs=[pl.BlockSpec((MOE_ROWS, d), lambda j, b, be, nv: (blk(b, nv), 0)),
                  pl.BlockSpec((1, d, tn), lambda j, b, be, nv: (be[blk(b, nv)], 0, j)),
                  pl.BlockSpec((1, d, tn), lambda j, b, be, nv: (be[blk(b, nv)], 0, j))],
        out_specs=pl.BlockSpec((MOE_ROWS, tn), lambda j, b, be, nv: (b, j)),
        scratch_shapes=[pltpu.VMEM((d, tn), BF16), pltpu.VMEM((d, tn), BF16)],
    )
    return pl.pallas_call(
        _moe_glu_kernel,
        out_shape=jax.ShapeDtypeStruct((n, f), BF16),
        grid_spec=grid_spec,
        compiler_params=_cparams(("arbitrary", "arbitrary")),
        name="moe_glu",
    )(block_e, n_valid, rows_x, wg, wu)


def _moe_down_kernel(be_ref, nv_ref, x_ref, w_ref, o_ref):
    k = pl.program_id(2)

    used = pl.program_id(1) < nv_ref[0]

    @pl.when(used)
    def _():
        part = jnp.dot(x_ref[...], w_ref[0], preferred_element_type=F32)

        @pl.when(k == 0)
        def _():
            o_ref[...] = part

        @pl.when(k > 0)
        def _():
            o_ref[...] += part

    @pl.when(jnp.logical_not(used))
    def _():
        o_ref[...] = jnp.zeros_like(o_ref)


def _moe_down(act, wd, block_e, n_valid, *, tn, tk):
    n, f = act.shape
    nb = n // MOE_ROWS
    d = wd.shape[2]
    blk = lambda b, nv: jnp.minimum(b, nv[0] - 1)
    grid_spec = pltpu.PrefetchScalarGridSpec(
        num_scalar_prefetch=2,
        grid=(d // tn, nb, f // tk),
        in_specs=[pl.BlockSpec((MOE_ROWS, tk), lambda j, b, k, be, nv: (blk(b, nv), k)),
                  pl.BlockSpec((1, tk, tn), lambda j, b, k, be, nv: (be[blk(b, nv)], k, j))],
        out_specs=pl.BlockSpec((MOE_ROWS, tn), lambda j, b, k, be, nv: (b, j)),
    )
    return pl.pallas_call(
        _moe_down_kernel,
        out_shape=jax.ShapeDtypeStruct((n, d), F32),
        grid_spec=grid_spec,
        compiler_params=_cparams(("arbitrary", "arbitrary", "arbitrary")),
        name="moe_down",
    )(block_e, n_valid, act, wd)


def _combine_kernel(p0_ref, p1_ref, n0_ref, n1_ref, x_ref, gate_ref, out_hbm, o_ref, buf, sem, *, rows):
    i = pl.program_id(0)
    slot = i % 2

    def row_copy(src_row, s, k, r):
        return pltpu.make_async_copy(out_hbm.at[pl.ds(src_row, 1)], buf.at[s, k, pl.ds(r, 1)], sem.at[s, k])

    def issue(ref0, ref1, s):
        def start(r, carry):
            row_copy(ref0[0, 0, r], s, 0, r).start()
            row_copy(ref1[0, 0, r], s, 1, r).start()
            return carry

        lax.fori_loop(0, rows, start, 0, unroll=8)

    @pl.when(i == 0)
    def _():
        issue(p0_ref, p1_ref, 0)

    @pl.when(i + 1 < pl.num_programs(0))
    def _():
        issue(n0_ref, n1_ref, 1 - slot)

    def wait(r, carry):
        row_copy(0, slot, 0, r).wait()
        row_copy(0, slot, 1, r).wait()
        return carry

    lax.fori_loop(0, rows, wait, 0, unroll=8)
    gate = gate_ref[...]
    g0 = gate[:, 2:3]
    g1 = gate[:, 3:4]
    o_ref[...] = x_ref[...] + (buf[slot, 0] * g0 + buf[slot, 1] * g1)


def _combine(x, route, out_rows, pos0, pos1, *, rows):
    m, d = x.shape
    nb = m // rows
    idx_spec = pl.BlockSpec((1, 1, rows), lambda i: (i, 0, 0), memory_space=pltpu.SMEM)
    nxt_spec = pl.BlockSpec((1, 1, rows), lambda i: (jnp.minimum(i + 1, nb - 1), 0, 0), memory_space=pltpu.SMEM)
    p0 = pos0.reshape(nb, 1, rows)
    p1 = pos1.reshape(nb, 1, rows)
    return pl.pallas_call(
        functools.partial(_combine_kernel, rows=rows),
        out_shape=jax.ShapeDtypeStruct((m, d), F32),
        grid=(nb,),
        in_specs=[idx_spec, idx_spec, nxt_spec, nxt_spec, pl.BlockSpec((rows, d), lambda i: (i, 0)),
                  pl.BlockSpec((rows, LANES), lambda i: (i, 0)), pl.BlockSpec(memory_space=pl.ANY)],
        out_specs=pl.BlockSpec((rows, d), lambda i: (i, 0)),
        scratch_shapes=[pltpu.VMEM((2, TOP_K, rows, d), F32), pltpu.SemaphoreType.DMA((2, TOP_K))],
        compiler_params=_cparams(("arbitrary",)),
        name="moe_combine",
    )(p0, p1, p0, p1, x, route, out_rows)


def _moe_layer(x, g, router, wg, wu, wd, *, tm, glu_tn, down_tn, down_tk):
    m, d = x.shape
    h, route = _norm_router(x, g, router, tm=tm)
    eid = route[:, :TOP_K].astype(jnp.int32).reshape(-1)
    n_assign = m * TOP_K
    onehot = (eid[:, None] == jnp.arange(N_EXPERTS)[None, :]).astype(jnp.int32)
    rank = jnp.take_along_axis(jnp.cumsum(onehot, axis=0), eid[:, None], axis=1)[:, 0] - 1
    counts = jnp.sum(onehot, axis=0)
    padded = (counts + MOE_ROWS - 1) // MOE_ROWS * MOE_ROWS
    pend = jnp.cumsum(padded)
    pstart = pend - padded
    dest = pstart[eid] + rank
    nb = -(-n_assign // MOE_ROWS) + N_EXPERTS
    tok = jnp.arange(n_assign, dtype=jnp.int32) // TOP_K
    src_tok = jnp.zeros((nb * MOE_ROWS,), jnp.int32).at[dest].set(tok)
    block_e = jnp.minimum(jnp.searchsorted(pend, jnp.arange(nb) * MOE_ROWS, side="right"), N_EXPERTS - 1)
    block_e = block_e.astype(jnp.int32)
    n_valid = (pend[-1] // MOE_ROWS).astype(jnp.int32).reshape(1)
    rows_x = _gather_rows(h, src_tok, BF16, rows=MOE_ROWS)
    act = _moe_glu(rows_x, wg, wu, block_e, n_valid, tn=glu_tn)
    out_rows = _moe_down(act, wd, block_e, n_valid, tn=down_tn, tk=down_tk)
    dest2 = dest.reshape(m, TOP_K).astype(jnp.int32)
    return _combine(x, route, out_rows, dest2[:, 0], dest2[:, 1], rows=256)


def kernel(x_prompt, x_sample, state_hgrn, state_rwkv, state_rwkv_shift, cache_swa_k, cache_swa_v, norm_mix, w_in, hg_lower_bounds, hg_norm, rw_mu, rw_w0, rw_w2, rw_a0, rw_a2, rw_g2, rw_kk, rw_ka, rw_rk, rw_lnw, rw_lnb, sw_sinks, w_branch, w_gate_br, w_out, norm_ffn, ffn_w_gate, ffn_w_up, ffn_w_down, moe_router, moe_w_gate, moe_w_up, moe_w_down, norm_final):
    bp, tp, d = x_prompt.shape
    bs, ts, _ = x_sample.shape
    mp, ms = bp * tp, bs * ts
    m = mp + ms
    depth = w_in.shape[0]
    tm = 768
    tm_norm = 256
    x = jnp.concatenate([x_prompt.reshape(mp, d), x_sample.reshape(ms, d)], axis=0)

    lbs = jax.nn.softmax(hg_lower_bounds.astype(F32), axis=0)
    lbs = jnp.cumsum(lbs, axis=0) - lbs[0]
    cos_p, sin_p = _rope_tables(jnp.arange(tp))
    cos_s, sin_s = _rope_tables(PAST_LEN + jnp.arange(ts))
    kvw = SW_KV_HEADS * SW_HD

    outs = {k: [] for k in ("hg_p", "hg_s", "rw_p", "rw_s", "sh_p", "sh_s", "k_p", "v_p", "k_s", "v_s")}
    for l in range(depth):
        h = _rmsnorm(x, norm_mix[l], BF16, tm_norm)
        w_in_l = w_in[l]
        p_hg = _matmul(h, w_in_l[:, :HG_IN].astype(BF16), tm=tm, tn=1024, name="proj_hg")
        p_rw = _matmul(h, w_in_l[:, HG_IN:HG_IN + RW_IN].astype(BF16), tm=tm, tn=RW_IN // 2, name="proj_rw")
        p_sw = _matmul(h, w_in_l[:, HG_IN + RW_IN:].astype(BF16), tm=tm, tn=SW_IN, name="proj_sw")

        y_hg_p, s_hg_p = _hgrn(p_hg, lbs[l], hg_norm[l], jnp.zeros((bp, HG_HEADS, HG_DK, HG_DV), F32),
                               batch=bp, seq=tp, row_off=0, tc=256)
        y_hg_s, s_hg_s = _hgrn(p_hg, lbs[l], hg_norm[l], state_hgrn[l], batch=bs, seq=ts, row_off=mp, tc=ts)
        rw_prm = dict(mu=rw_mu[l], w0=rw_w0[l], w2=rw_w2[l], a0=rw_a0[l], a2=rw_a2[l], g2=rw_g2[l], kk=rw_kk[l],
                      ka=rw_ka[l], rk=rw_rk[l], lnw=rw_lnw[l], lnb=rw_lnb[l])
        y_rw_p, s_rw_p = _rwkv(p_rw, jnp.zeros((bp, RW_IN), F32), jnp.zeros((bp, RW_HEADS, RW_HD, RW_HD), F32),
                               rw_prm, batch=bp, seq=tp, row_off=0, nsub=8)
        y_rw_s, s_rw_s = _rwkv(p_rw, state_rwkv_shift[l], state_rwkv[l], rw_prm, batch=bs, seq=ts, row_off=mp, nsub=1)
        past_s = jnp.concatenate([cache_swa_k[l].reshape(bs, WINDOW, kvw), cache_swa_v[l].reshape(bs, WINDOW, kvw)],
                                 axis=-1)
        y_sw_p, kv_p = _swa(p_sw, cos_p, sin_p, jnp.zeros((bp, WINDOW, 2 * kvw), F32), sw_sinks[l],
                            batch=bp, seq=tp, row_off=0, past_valid=False)
        y_sw_s, kv_s = _swa(p_sw, cos_s, sin_s, past_s, sw_sinks[l], batch=bs, seq=ts, row_off=mp, past_valid=True)

        branches = [jnp.concatenate([a, b], axis=0) for a, b in ((y_hg_p, y_hg_s), (y_rw_p, y_rw_s), (y_sw_p, y_sw_s))]
        merged = _gate_merge(h, w_gate_br[l].astype(BF16), branches, w_branch[l].astype(BF16), tm=tm, tn=256)
        x = _matmul(merged, w_out[l].astype(BF16), tm=tm, tn=1024, res=x, name="w_out")

        j = l // 2
        if l % 2 == 0:
            h2 = _rmsnorm(x, norm_ffn[l], BF16, tm_norm)
            act = _glu(h2, ffn_w_gate[j].astype(BF16), ffn_w_up[j].astype(BF16), tm=tm, tn=512)
            x = _matmul(act, ffn_w_down[j].astype(BF16), tm=tm, tn=1024, tk=3584, res=x, name="ffn_down")
        else:
            x = _moe_layer(x, norm_ffn[l], moe_router[j], moe_w_gate[j], moe_w_up[j],
                           moe_w_down[j].astype(BF16), tm=tm_norm, glu_tn=512, down_tn=2048, down_tk=3584)

        outs["hg_p"].append(s_hg_p)
        outs["hg_s"].append(s_hg_s)
        outs["rw_p"].append(s_rw_p)
        outs["rw_s"].append(s_rw_s)
        outs["sh_p"].append(p_rw[tp - 1:mp:tp])
        outs["sh_s"].append(p_rw[mp + ts - 1::ts])
        kv_p = kv_p.reshape(bp, tp, 2, SW_KV_HEADS, SW_HD)[:, -WINDOW:]
        kv_s = kv_s.reshape(bs, ts, 2, SW_KV_HEADS, SW_HD)
        outs["k_p"].append(kv_p[:, :, 0])
        outs["v_p"].append(kv_p[:, :, 1])
        outs["k_s"].append(kv_s[:, :, 0])
        outs["v_s"].append(kv_s[:, :, 1])

    y_p = _rmsnorm(x, norm_final, F32, tm_norm, 0, mp)
    y_s = _rmsnorm(x, norm_final, F32, tm_norm, mp, ms)
    st = {k: jnp.stack(v) for k, v in outs.items()}
    return (y_p.reshape(bp, tp, d), y_s.reshape(bs, ts, d), st["hg_p"], st["hg_s"], st["rw_p"], st["rw_s"],
            st["sh_p"], st["sh_s"], st["k_p"], st["v_p"], st["k_s"], st["v_s"])
```
